```python
import math
import jax, jax.numpy as jnp
from jax import lax
import numpy as np

D_MODEL = 2048
BATCH = 8
SEQ = 2048
DEPTH = 1

CHUNK = 64
HEAD_DIM = 128
ATTN_WIDTH = D_MODEL // 2
ATTN_HEADS = ATTN_WIDTH // HEAD_DIM
SSM_WIDTH = D_MODEL - ATTN_WIDTH
SSM_GROUP = 16
SSM_GROUPS = SSM_WIDTH // SSM_GROUP
SSM_STATE = 64
IN_WIDTH = 3 * ATTN_WIDTH + SSM_WIDTH
N_BACK_CHUNKS = 8
BAND = (N_BACK_CHUNKS + 1) * CHUNK
REL_CLIP = 128
N_REL = 2 * REL_CLIP + 1
N_EXPERTS = 32
TOP_K = 4
D_FF = D_MODEL
SWIGLU_LIMIT = 7.0
SWIGLU_ALPHA = 1.702
DT_MIN = 0.001
DT_MAX = 0.1
EPS = 1e-5

kernel_name = "hybrid_s5_chunkattn_moe_block"


def _rmsnorm(x, g):
    xf = x.astype(jnp.float32)
    r = lax.rsqrt(jnp.mean(xf * xf, axis=-1, keepdims=True) + EPS)
    return (xf * r * g.astype(jnp.float32)).astype(x.dtype)


def _chunk_attention(q, k, v, rel_bias):
    bsz, L, _ = q.shape
    nc = L // CHUNK
    q = q.reshape(bsz, nc, CHUNK, ATTN_HEADS, HEAD_DIM)
    k = k.reshape(bsz, nc, CHUNK, ATTN_HEADS, HEAD_DIM)
    v = v.reshape(bsz, nc, CHUNK, ATTN_HEADS, HEAD_DIM)
    pad = ((0, 0), (N_BACK_CHUNKS, 0), (0, 0), (0, 0), (0, 0))
    kp = jnp.pad(k, pad)
    vp = jnp.pad(v, pad)
    band_idx = jnp.arange(nc)[:, None] + jnp.arange(N_BACK_CHUNKS + 1)[None, :]
    kb = kp[:, band_idx].reshape(bsz, nc, BAND, ATTN_HEADS, HEAD_DIM)
    vb = vp[:, band_idx].reshape(bsz, nc, BAND, ATTN_HEADS, HEAD_DIM)
    valid = jnp.repeat(band_idx >= N_BACK_CHUNKS, CHUNK, axis=1)
    rel = N_BACK_CHUNKS * CHUNK + jnp.arange(CHUNK)[:, None] - jnp.arange(BAND)[None, :]
    rel_idx = jnp.clip(rel, -REL_CLIP, REL_CLIP) + REL_CLIP
    bias = rel_bias.astype(jnp.float32)[:, rel_idx]
    scale = 1.0 / math.sqrt(HEAD_DIM)
    s = jnp.einsum('bnqhd,bnkhd->bhnqk', q, kb, preferred_element_type=jnp.float32) * scale
    s = s + bias[None, :, None]
    s = jnp.where(valid[None, None, :, None, :], s, jnp.float32(-1e30))
    p = jax.nn.softmax(s, axis=-1)
    o = jnp.einsum('bhnqk,bnkhd->bnqhd', p.astype(v.dtype), vb)
    return o.reshape(bsz, L, ATTN_WIDTH)


def _s5(u, lam_re, lam_im, log_step, b_re, b_im, c_re, c_im, d, w_glu, b_glu):
    f32 = jnp.float32
    bsz, L, _ = u.shape
    uf = u.astype(f32)
    ug = uf.reshape(bsz, L, SSM_GROUPS, SSM_GROUP)
    dt = jnp.exp(log_step.astype(f32))[:, None]
    lr = lam_re.astype(f32)
    li = lam_im.astype(f32)
    mag = jnp.exp(lr * dt)
    ab_re = mag * jnp.cos(li * dt)
    ab_im = mag * jnp.sin(li * dt)
    den = lr * lr + li * li
    nr = ab_re - 1.0
    ni = ab_im
    f_re = (nr * lr + ni * li) / den
    f_im = (ni * lr - nr * li) / den
    br = b_re.astype(f32)
    bi = b_im.astype(f32)
    bb_re = f_re[..., None] * br - f_im[..., None] * bi
    bb_im = f_re[..., None] * bi + f_im[..., None] * br
    x_re = jnp.einsum('blgc,gpc->blgp', ug, bb_re)
    x_im = jnp.einsum('blgc,gpc->blgp', ug, bb_im)
    a_re = jnp.broadcast_to(ab_re, (1, L, SSM_GROUPS, SSM_STATE))
    a_im = jnp.broadcast_to(ab_im, (1, L, SSM_GROUPS, SSM_STATE))

    def combine(e1, e2):
        a1r, a1i, b1r, b1i = e1
        a2r, a2i, b2r, b2i = e2
        return (a2r * a1r - a2i * a1i,
                a2r * a1i + a2i * a1r,
                a2r * b1r - a2i * b1i + b2r,
                a2r * b1i + a2i * b1r + b2i)

    _, _, h_re, h_im = lax.associative_scan(combine, (a_re, a_im, x_re, x_im), axis=1)
    y = (jnp.einsum('blgp,gcp->blgc', h_re, c_re.astype(f32))
         - jnp.einsum('blgp,gcp->blgc', h_im, c_im.astype(f32)))
    y = y.reshape(bsz, L, SSM_WIDTH) + d.astype(f32) * uf
    y = jax.nn.gelu(y)
    y = y * jax.nn.sigmoid(y @ w_glu.astype(f32) + b_glu.astype(f32))
    return y.astype(u.dtype)


def _moe(xn, w_router, b_router, w_gate, b_gate, w_up, b_up, w_down, b_down):
    bsz, L, D = xn.shape
    t = xn.reshape(-1, D)
    logits = (t @ w_router).astype(jnp.float32) + b_router.astype(jnp.float32)
    top_vals, top_idx = lax.top_k(logits, TOP_K)
    gates = jax.nn.softmax(top_vals, axis=-1)
    comb = jnp.sum(jax.nn.one_hot(top_idx, N_EXPERTS, dtype=jnp.float32) * gates[..., None], axis=1)
    out = jnp.zeros((t.shape[0], D), jnp.float32)
    for e in range(N_EXPERTS):
        g = jnp.minimum(t @ w_gate[e] + b_gate[e], SWIGLU_LIMIT)
        u = jnp.clip(t @ w_up[e] + b_up[e], -SWIGLU_LIMIT, SWIGLU_LIMIT)
        hdn = g * jax.nn.sigmoid(SWIGLU_ALPHA * g) * (u + 1.0)
        y = hdn @ w_down[e] + b_down[e]
        out = out + comb[:, e:e + 1] * y.astype(jnp.float32)
    return out.astype(xn.dtype).reshape(bsz, L, D)


def setup_inputs(seed: int = 0) -> dict:
    key = jax.random.key(seed)
    ks = jax.random.split(key, 32)
    f32 = jnp.float32
    nrm = lambda k, shp, s: jax.random.normal(k, shp, f32) * s
    lam_im_base = jnp.pi * jnp.arange(SSM_STATE, dtype=f32)
    return {
        "x": nrm(ks[0], (BATCH, SEQ, D_MODEL), 1.0),
        "norm1_g": 1.0 + nrm(ks[1], (DEPTH, D_MODEL), 0.01),
        "w_in": nrm(ks[2], (DEPTH, D_MODEL, IN_WIDTH), D_MODEL ** -0.5),
        "rel_bias": nrm(ks[3], (DEPTH, ATTN_HEADS, N_REL), 0.1),
        "ssm_lambda_re": -0.5 + nrm(ks[4], (DEPTH, SSM_GROUPS, SSM_STATE), 0.01),
        "ssm_lambda_im": lam_im_base + nrm(ks[5], (DEPTH, SSM_GROUPS, SSM_STATE), 0.01),
        "ssm_log_step": jax.random.uniform(ks[6], (DEPTH, SSM_GROUPS), f32, math.log(DT_MIN), math.log(DT_MAX)),
        "ssm_b_re": nrm(ks[7], (DEPTH, SSM_GROUPS, SSM_STATE, SSM_GROUP), (2 * SSM_GROUP) ** -0.5),
        "ssm_b_im": nrm(ks[8], (DEPTH, SSM_GROUPS, SSM_STATE, SSM_GROUP), (2 * SSM_GROUP) ** -0.5),
        "ssm_c_re": nrm(ks[9], (DEPTH, SSM_GROUPS, SSM_GROUP, SSM_STATE), (2 * SSM_STATE) ** -0.5),
        "ssm_c_im": nrm(ks[10], (DEPTH, SSM_GROUPS, SSM_GROUP, SSM_STATE), (2 * SSM_STATE) ** -0.5),
        "ssm_d": nrm(ks[11], (DEPTH, SSM_WIDTH), 1.0),
        "w_ssm_glu": nrm(ks[12], (DEPTH, SSM_WIDTH, SSM_WIDTH), SSM_WIDTH ** -0.5),
        "b_ssm_glu": nrm(ks[13], (DEPTH, SSM_WIDTH), 0.01),
        "attn_out_g": 1.0 + nrm(ks[14], (DEPTH, ATTN_WIDTH), 0.01),
        "ssm_out_g": 1.0 + nrm(ks[15], (DEPTH, SSM_WIDTH), 0.01),
        "w_out": nrm(ks[16], (DEPTH, D_MODEL, D_MODEL), D_MODEL ** -0.5),
        "norm2_g": 1.0 + nrm(ks[17], (DEPTH, D_MODEL), 0.01),
        "w_router": nrm(ks[18], (DEPTH, D_MODEL, N_EXPERTS), D_MODEL ** -0.5),
        "b_router": nrm(ks[19], (DEPTH, N_EXPERTS), 0.01),
        "w_gate": nrm(ks[20], (DEPTH, N_EXPERTS, D_MODEL, D_FF), D_MODEL ** -0.5),
        "b_gate": nrm(ks[21], (DEPTH, N_EXPERTS, D_FF), 0.01),
        "w_up": nrm(ks[22], (DEPTH, N_EXPERTS, D_MODEL, D_FF), D_MODEL ** -0.5),
        "b_up": nrm(ks[23], (DEPTH, N_EXPERTS, D_FF), 0.01),
        "w_down": nrm(ks[24], (DEPTH, N_EXPERTS, D_FF, D_MODEL), D_FF ** -0.5),
        "b_down": nrm(ks[25], (DEPTH, N_EXPERTS, D_MODEL), 0.01),
        "norm_f_g": 1.0 + nrm(ks[26], (D_MODEL,), 0.01),
    }


def reference(x, norm1_g, w_in, rel_bias, ssm_lambda_re, ssm_lambda_im, ssm_log_step,
              ssm_b_re, ssm_b_im, ssm_c_re, ssm_c_im, ssm_d, w_ssm_glu, b_ssm_glu,
              attn_out_g, ssm_out_g, w_out, norm2_g, w_router, b_router,
              w_gate, b_gate, w_up, b_up, w_down, b_down, norm_f_g):
    h = x
    for l in range(DEPTH):
        n = _rmsnorm(h, norm1_g[l])
        proj = n @ w_in[l]
        q = proj[..., :ATTN_WIDTH]
        k = proj[..., ATTN_WIDTH:2 * ATTN_WIDTH]
        v = proj[..., 2 * ATTN_WIDTH:3 * ATTN_WIDTH]
        u = proj[..., 3 * ATTN_WIDTH:]
        a_out = _chunk_attention(q, k, v, rel_bias[l])
        s_out = _s5(u, ssm_lambda_re[l], ssm_lambda_im[l], ssm_log_step[l],
                    ssm_b_re[l], ssm_b_im[l], ssm_c_re[l], ssm_c_im[l], ssm_d[l],
                    w_ssm_glu[l], b_ssm_glu[l])
        mixed = jnp.concatenate([_rmsnorm(a_out, attn_out_g[l]),
                                 _rmsnorm(s_out, ssm_out_g[l])], axis=-1)
        h = h + mixed @ w_out[l]
        n2 = _rmsnorm(h, norm2_g[l])
        h = h + _moe(n2, w_router[l], b_router[l], w_gate[l], b_gate[l],
                     w_up[l], b_up[l], w_down[l], b_down[l])
    return _rmsnorm(h, norm_f_g)
```

```python
import functools
import math

import jax
import jax.numpy as jnp
from jax import lax
from jax.experimental import pallas as pl
from jax.experimental.pallas import tpu as pltpu

F32 = jnp.float32
BF16 = jnp.bfloat16
I32 = jnp.int32

CHUNK = 64
HEAD_DIM = 128
N_BACK_CHUNKS = 8
REL_CLIP = 128
SSM_GROUP = 16
SSM_STATE = 64
TOP_K = 4
SWIGLU_LIMIT = 7.0
SWIGLU_ALPHA = 1.702
EPS = 1e-5
NEG_BIG = -1e30

LANES = 128
SUBLANES = 8
MXU_DIM = 256
VMEM_LIMIT_BYTES = 56 * 1024 * 1024

ATT_TQ = 2 * CHUNK
ATT_WIN = (N_BACK_CHUNKS + 2) * CHUNK
ATT_NSHIFT = (N_BACK_CHUNKS * CHUNK) // ATT_TQ + 1

S5_GROUPS_PER_BLOCK = MXU_DIM // SSM_GROUP
S5_BLOCK_STATES = S5_GROUPS_PER_BLOCK * SSM_STATE
S5_STRIP = 512
S5_TC = 32

MOE_TM = 256
MOE_NTS = 4
MOE_TF = 512


def _cparams(sem):
    return pltpu.CompilerParams(dimension_semantics=sem, vmem_limit_bytes=VMEM_LIMIT_BYTES)


def _rms(x, g):
    r = lax.rsqrt(jnp.mean(x * x, axis=-1, keepdims=True) + EPS)
    return x * r * g


def _inproj_kernel(x_ref, g_ref, w_ref, o_ref, nb_ref):
    @pl.when(pl.program_id(1) == 0)
    def _():
        nb_ref[...] = _rms(x_ref[...], g_ref[...]).astype(BF16)

    o_ref[...] = jnp.dot(nb_ref[...], w_ref[...], preferred_element_type=F32).astype(o_ref.dtype)


def _inproj(x2d, g, w_bf16, tm, tn):
    t, d = x2d.shape
    n = w_bf16.shape[1]
    return pl.pallas_call(
        _inproj_kernel,
        grid=(t // tm, n // tn),
        in_specs=[
            pl.BlockSpec((tm, d), lambda i, j: (i, 0)),
            pl.BlockSpec((1, d), lambda i, j: (0, 0)),
            pl.BlockSpec((d, tn), lambda i, j: (0, j)),
        ],
        out_specs=pl.BlockSpec((tm, tn), lambda i, j: (i, j)),
        out_shape=jax.ShapeDtypeStruct((t, n), BF16),
        scratch_shapes=[pltpu.VMEM((tm, d), BF16)],
        compiler_params=_cparams(("arbitrary", "arbitrary")),
        name="inproj",
    )(x2d, g.reshape(1, d), w_bf16)


def _attn_bias_table(rel_bias):
    r = jnp.arange(ATT_TQ)[None, :, None]
    j = jnp.arange(ATT_WIN)[None, None, :]
    sh = jnp.arange(ATT_NSHIFT)[:, None, None]
    rel = N_BACK_CHUNKS * CHUNK - ATT_TQ * sh + r - j
    idx = jnp.clip(rel, -REL_CLIP, REL_CLIP) + REL_CLIP
    return jnp.transpose(rel_bias.astype(F32)[:, idx], (1, 0, 2, 3))


def _attn_kernel(q_ref, k_ref, v_ref, b_ref, g_ref, o_ref, a_ref, *, n_heads):
    i = pl.program_id(1)
    s0 = pl.multiple_of(jnp.maximum(i * ATT_TQ - N_BACK_CHUNKS * CHUNK, 0), ATT_TQ)
    row = lax.broadcasted_iota(I32, (ATT_TQ, ATT_WIN), 0)
    col = lax.broadcasted_iota(I32, (ATT_TQ, ATT_WIN), 1)
    qc = (i * ATT_TQ + row) // CHUNK
    kc = (s0 + col) // CHUNK
    ok = (kc <= qc) & (kc >= qc - N_BACK_CHUNKS)
    scale = 1.0 / math.sqrt(HEAD_DIM)
    for h in range(n_heads):
        hs = slice(h * HEAD_DIM, (h + 1) * HEAD_DIM)
        qh = q_ref[:, hs]
        kh = k_ref[pl.ds(s0, ATT_WIN), hs]
        vh = v_ref[pl.ds(s0, ATT_WIN), hs]
        s = lax.dot_general(qh, kh, (((1,), (1,)), ((), ())), preferred_element_type=F32)
        s = jnp.where(ok, s * scale + b_ref[h], NEG_BIG)
        m = jnp.max(s, axis=-1, keepdims=True)
        p = jnp.exp(s - m)
        l = jnp.sum(p, axis=-1, keepdims=True)
        o = jnp.dot(p.astype(BF16), vh, preferred_element_type=F32)
        a_ref[:, hs] = o / l
    o_ref[...] = _rms(a_ref[...], g_ref[...]).astype(o_ref.dtype)


def _attention(proj3, bias_tbl, g, n_heads):
    b, l, _ = proj3.shape
    aw = n_heads * HEAD_DIM
    nsh = ATT_NSHIFT
    return pl.pallas_call(
        functools.partial(_attn_kernel, n_heads=n_heads),
        grid=(b, l // ATT_TQ),
        in_specs=[
            pl.BlockSpec((None, ATT_TQ, aw), lambda bi, i: (bi, i, 0)),
            pl.BlockSpec((None, l, aw), lambda bi, i: (bi, 0, 1)),
            pl.BlockSpec((None, l, aw), lambda bi, i: (bi, 0, 2)),
            pl.BlockSpec((None, n_heads, ATT_TQ, ATT_WIN),
                         lambda bi, i: (jnp.maximum(nsh - 1 - i, 0), 0, 0, 0)),
            pl.BlockSpec((1, aw), lambda bi, i: (0, 0)),
        ],
        out_specs=pl.BlockSpec((None, ATT_TQ, aw), lambda bi, i: (bi, i, 0)),
        out_shape=jax.ShapeDtypeStruct((b, l, aw), BF16),
        scratch_shapes=[pltpu.VMEM((ATT_TQ, aw), F32)],
        compiler_params=_cparams(("arbitrary", "arbitrary")),
        name="attn",
    )(proj3, proj3, proj3, bias_tbl, g.reshape(1, aw))


def _s5_params(lam_re, lam_im, log_step, b_re, b_im, c_re, c_im):
    g, p = lam_re.shape
    nb = g // S5_GROUPS_PER_BLOCK
    gl = S5_GROUPS_PER_BLOCK
    dt = jnp.exp(log_step.astype(F32))[:, None]
    lr = lam_re.astype(F32)
    li = lam_im.astype(F32)
    mag = jnp.exp(lr * dt)
    ab_re = mag * jnp.cos(li * dt)
    ab_im = mag * jnp.sin(li * dt)
    den = lr * lr + li * li
    nr = ab_re - 1.0
    ni = ab_im
    f_re = (nr * lr + ni * li) / den
    f_im = (ni * lr - nr * li) / den
    br = b_re.astype(F32)
    bi = b_im.astype(F32)
    bb_re = f_re[..., None] * br - f_im[..., None] * bi
    bb_im = f_re[..., None] * bi + f_im[..., None] * br
    eye = jnp.eye(gl, dtype=F32)
    bb = jnp.stack([bb_re, bb_im]).reshape(2, nb, gl, p, SSM_GROUP)
    wbd = jnp.einsum("ajgpc,gh->jgcahp", bb, eye).reshape(nb, gl * SSM_GROUP, 2 * gl * p)
    cc = jnp.stack([c_re.astype(F32), -c_im.astype(F32)]).reshape(2, nb, gl, SSM_GROUP, p)
    cbd = jnp.einsum("ajgcp,gh->jagphc", cc, eye).reshape(nb, 2 * gl * p, gl * SSM_GROUP)
    avec = jnp.stack([ab_re, ab_im]).reshape(2, nb, gl, p)
    avec = jnp.transpose(avec, (1, 0, 2, 3)).reshape(1, 2 * g * p)
    return wbd.astype(BF16), cbd.astype(BF16), avec


def _s5_kernel(u_ref, pm_ref, pmt_ref, wbd_ref, cbd_ref, av_ref, d_ref, wglu_ref, bglu_ref,
               g_ref, o_ref, xs_ref, hst_ref, *, n_blocks, tc, batch):
    rows = tc * batch
    sw = u_ref.shape[-1]
    bw = 2 * S5_BLOCK_STATES
    cw = S5_GROUPS_PER_BLOCK * SSM_GROUP

    @pl.when(pl.program_id(0) == 0)
    def _():
        hst_ref[...] = jnp.zeros_like(hst_ref)

    u_tm = jnp.dot(pm_ref[...], u_ref[...].reshape(rows, sw), preferred_element_type=F32)
    u_tmb = u_tm.astype(BF16)
    for j in range(n_blocks):
        xs_ref[:, j * bw:(j + 1) * bw] = jnp.dot(
            u_tmb[:, j * cw:(j + 1) * cw], wbd_ref[j], preferred_element_type=F32)

    for j in range(n_blocks):
        for s in range(S5_BLOCK_STATES // S5_STRIP):
            cr = j * bw + s * S5_STRIP
            ci = cr + S5_BLOCK_STATES
            a_re = jnp.broadcast_to(av_ref[:, cr:cr + S5_STRIP], (batch, S5_STRIP))
            a_im = jnp.broadcast_to(av_ref[:, ci:ci + S5_STRIP], (batch, S5_STRIP))

            def step(t, carry, cr=cr, ci=ci, a_re=a_re, a_im=a_im):
                h_re, h_im = carry
                r0 = pl.multiple_of(t * batch, batch)
                n_re = a_re * h_re - a_im * h_im + xs_ref[pl.ds(r0, batch), cr:cr + S5_STRIP]
                n_im = a_re * h_im + a_im * h_re + xs_ref[pl.ds(r0, batch), ci:ci + S5_STRIP]
                xs_ref[pl.ds(r0, batch), cr:cr + S5_STRIP] = n_re
                xs_ref[pl.ds(r0, batch), ci:ci + S5_STRIP] = n_im
                return n_re, n_im

            h_re, h_im = lax.fori_loop(
                0, tc, step, (hst_ref[:, cr:cr + S5_STRIP], hst_ref[:, ci:ci + S5_STRIP]), unroll=4)
            hst_ref[:, cr:cr + S5_STRIP] = h_re
            hst_ref[:, ci:ci + S5_STRIP] = h_im

    ys = [jnp.dot(xs_ref[:, j * bw:(j + 1) * bw].astype(BF16), cbd_ref[j], preferred_element_type=F32)
          for j in range(n_blocks)]
    y = jnp.concatenate(ys, axis=1) + d_ref[...] * u_tm
    y = jax.nn.gelu(y)
    z = jnp.dot(y.astype(BF16), wglu_ref[...], preferred_element_type=F32) + bglu_ref[...]
    y = y * jax.nn.sigmoid(z)
    res = _rms(y, g_ref[...]).astype(BF16)
    out_bm = jnp.dot(pmt_ref[...], res, preferred_element_type=F32).astype(o_ref.dtype)
    o_ref[...] = out_bm.reshape(batch, tc, sw)


def _s5(proj3, wbd, cbd, avec, d, wglu_bf16, bglu, g, u_block_index):
    b, l, _ = proj3.shape
    assert b == SUBLANES, "the scan keeps the batch on the sublane axis"
    n_blocks, cw, bw = wbd.shape
    sw = n_blocks * cw
    tc = S5_TC
    rows = tc * b
    r = jnp.arange(rows)
    src = (r % b) * tc + r // b
    pm = (src[:, None] == jnp.arange(rows)[None, :]).astype(BF16)
    pmt = pm.T
    const = lambda shape: pl.BlockSpec(shape, lambda c: (0,) * len(shape))
    return pl.pallas_call(
        functools.partial(_s5_kernel, n_blocks=n_blocks, tc=tc, batch=b),
        grid=(l // tc,),
        in_specs=[
            pl.BlockSpec((b, tc, sw), lambda c: (0, c, u_block_index)),
            const((rows, rows)), const((rows, rows)),
            const((n_blocks, cw, bw)), const((n_blocks, bw, cw)),
            const((1, n_blocks * bw)), const((1, sw)),
            const((sw, sw)), const((1, sw)), const((1, sw)),
        ],
        out_specs=pl.BlockSpec((b, tc, sw), lambda c: (0, c, 0)),
        out_shape=jax.ShapeDtypeStruct((b, l, sw), BF16),
        scratch_shapes=[pltpu.VMEM((rows, n_blocks * bw), F32),
                        pltpu.VMEM((b, n_blocks * bw), F32)],
        compiler_params=_cparams(("arbitrary",)),
        name="s5",
    )(proj3, pm, pmt, wbd, cbd, avec, d.reshape(1, sw), wglu_bf16, bglu.reshape(1, sw),
      g.reshape(1, sw))


def _to_slab(dst_ref, src_ref, row0, n_groups):
    pieces = src_ref.shape[-1] // LANES

    def group(g, c):
        r0 = pl.multiple_of(g * SUBLANES, SUBLANES)
        for p in range(pieces):
            dst_ref[pl.ds((row0 + r0) * pieces + p, SUBLANES, stride=pieces), :] = (
                src_ref[pl.ds(r0, SUBLANES), p * LANES:(p + 1) * LANES])
        return c

    lax.fori_loop(0, n_groups, group, 0)


def _from_slab(dst_ref, slab_ref, row0, n_groups):
    pieces = dst_ref.shape[-1] // LANES
    rows = 2 * SUBLANES

    def group(g, c):
        r0 = pl.multiple_of(row0 + g * rows, rows)
        for p in range(pieces):
            lo = slab_ref[pl.ds(r0 * pieces + p, SUBLANES, stride=pieces), :]
            hi = slab_ref[pl.ds((r0 + SUBLANES) * pieces + p, SUBLANES, stride=pieces), :]
            dst_ref[pl.ds(r0, rows), p * LANES:(p + 1) * LANES] = (
                jnp.concatenate([lo, hi], axis=0).astype(dst_ref.dtype))
        return c

    lax.fori_loop(0, n_groups, group, 0)


def _outproj_kernel(ma_ref, ms_ref, x_ref, wo_ref, g2_ref, wr_ref, br_ref, tri_ref,
                    h_ref, n2s_ref, ri_ref, rf_ref, cnt_ref, carry_ref, n2_ref):
    aw = ma_ref.shape[-1]

    @pl.when(pl.program_id(0) == 0)
    def _():
        carry_ref[...] = jnp.zeros_like(carry_ref)

    h = (x_ref[...]
         + jnp.dot(ma_ref[...], wo_ref[:aw, :], preferred_element_type=F32)
         + jnp.dot(ms_ref[...], wo_ref[aw:, :], preferred_element_type=F32))
    h_ref[...] = h
    n2 = _rms(h, g2_ref[...])
    n2_ref[...] = n2
    _to_slab(n2s_ref, n2_ref, 0, n2.shape[0] // SUBLANES)

    n_hi = n2.astype(BF16)
    n_lo = (n2 - n_hi.astype(F32)).astype(BF16)
    w = wr_ref[...]
    w_hi = w.astype(BF16)
    w_lo = (w - w_hi.astype(F32)).astype(BF16)
    logits = (jnp.dot(n_hi, w_hi, preferred_element_type=F32)
              + jnp.dot(n_lo, w_hi, preferred_element_type=F32)
              + jnp.dot(n_hi, w_lo, preferred_element_type=F32)) + br_ref[...]

    tm, ne = logits.shape
    lane = lax.broadcasted_iota(I32, (tm, ne), 1)
    work = logits
    vals, idxs = [], []
    for _ in range(TOP_K):
        m = jnp.max(work, axis=-1, keepdims=True)
        ik = jnp.min(jnp.where(work == m, lane, ne), axis=-1, keepdims=True)
        vals.append(m)
        idxs.append(ik)
        work = jnp.where(lane == ik, -jnp.inf, work)
    es = [jnp.exp(v - vals[0]) for v in vals]
    den = es[0] + es[1] + es[2] + es[3]

    hot = jnp.zeros((tm, ne), F32)
    for ik in idxs:
        hot = hot + (lane == ik).astype(F32)
    before = jnp.dot(tri_ref[...], hot.astype(BF16), preferred_element_type=F32) + carry_ref[...]
    ri = jnp.zeros((tm, ne), I32)
    rf = jnp.zeros((tm, ne), F32)
    for k in range(TOP_K):
        rank = jnp.sum(jnp.where(lane == idxs[k], before, 0.0), axis=-1, keepdims=True)
        ri = jnp.where(lane == k, idxs[k], ri)
        ri = jnp.where(lane == TOP_K + k, rank.astype(I32), ri)
        rf = jnp.where(lane == k, es[k] / den, rf)
    ri_ref[...] = ri
    rf_ref[...] = rf
    carry_ref[...] = carry_ref[...] + jnp.sum(hot, axis=0, keepdims=True)
    cnt_ref[...] = carry_ref[...]


def _outproj(ma, ms, x2d, wo_bf16, g2, w_router, b_router, tm):
    t, d = x2d.shape
    aw = ma.shape[1]
    ne = w_router.shape[1]
    pieces = d // LANES
    wr = jnp.zeros((d, LANES), F32).at[:, :ne].set(w_router.astype(F32))
    br = jnp.full((1, LANES), NEG_BIG, F32).at[0, :ne].set(b_router.astype(F32))
    tri = (jnp.arange(tm)[:, None] > jnp.arange(tm)[None, :]).astype(BF16)
    const = lambda shape: pl.BlockSpec(shape, lambda i: (0,) * len(shape))
    row = lambda w: pl.BlockSpec((tm, w), lambda i: (i, 0))
    return pl.pallas_call(
        _outproj_kernel,
        grid=(t // tm,),
        in_specs=[row(aw), row(ms.shape[1]), row(d), const((d, d)), const((1, d)),
                  const((d, LANES)), const((1, LANES)), const((tm, tm))],
        out_specs=[row(d), pl.BlockSpec((tm * pieces, LANES), lambda i: (i, 0)),
                   row(LANES), row(LANES), const((1, LANES))],
        out_shape=[jax.ShapeDtypeStruct((t, d), F32), jax.ShapeDtypeStruct((t * pieces, LANES), F32),
                   jax.ShapeDtypeStruct((t, LANES), I32), jax.ShapeDtypeStruct((t, LANES), F32),
                   jax.ShapeDtypeStruct((1, LANES), F32)],
        scratch_shapes=[pltpu.VMEM((1, LANES), F32), pltpu.VMEM((tm, d), F32)],
        compiler_params=_cparams(("arbitrary",)),
        name="outproj",
    )(ma, ms, x2d, wo_bf16, g2.reshape(1, d), wr, br, tri)


def _route_tables(route_i, counts, n_tokens, n_experts):
    idx = route_i[:, :TOP_K]
    rank = route_i[:, TOP_K:2 * TOP_K]
    n_assign = n_tokens * TOP_K
    n_tiles_max = n_assign // MOE_TM + n_experts
    s_max = n_experts + n_tiles_max // MOE_NTS
    super_rows = MOE_NTS * MOE_TM

    tiles_e = (counts + MOE_TM - 1) // MOE_TM
    tile_off = jnp.cumsum(tiles_e) - tiles_e
    pos = (tile_off[idx] * MOE_TM + rank).reshape(-1)
    n_rows = (n_tiles_max + MOE_NTS) * MOE_TM
    row_token = jnp.zeros((n_rows,), I32).at[pos].set(
        jnp.repeat(jnp.arange(n_tokens, dtype=I32), TOP_K), unique_indices=True)
    a = jnp.arange(n_assign, dtype=I32)
    row_dst = jnp.zeros((n_rows,), I32).at[pos].set(
        (a % TOP_K) * n_tokens + a // TOP_K, unique_indices=True)

    nst_e = (tiles_e + MOE_NTS - 1) // MOE_NTS
    st_incl = jnp.cumsum(nst_e)
    num_st = st_incl[-1]
    s = jnp.arange(s_max, dtype=I32)
    active = s < num_st
    s_eff = jnp.minimum(s, num_st - 1)
    e_s = jnp.minimum(jnp.searchsorted(st_incl, s_eff, side="right"), n_experts - 1).astype(I32)
    local = s_eff - (st_incl[e_s] - nst_e[e_s])
    tile0 = (tile_off[e_s] + MOE_NTS * local).astype(I32)
    nt = jnp.where(active, jnp.minimum(MOE_NTS, tiles_e[e_s] - MOE_NTS * local), 0).astype(I32)
    nv = jnp.where(active, jnp.minimum(super_rows, counts[e_s] - super_rows * local), 0).astype(I32)
    tbl = lambda v: v.reshape(-1, MOE_TM // LANES, LANES)
    return e_s, tile0, nt, nv, tbl(row_token), tbl(row_dst)


def _moe_kernel(st_e, st_t0, st_nt, st_nv, tok_hbm, dst_hbm, n2_hbm,
                wg_ref, wu_ref, wd_ref, bg_ref, bu_ref, bd_ref, y_hbm,
                tok_s, dst_s, slab, xs_b, acc, wgb, wub, wdb, sem_i, sem_g, sem_s, *, n_j):
    s = pl.program_id(0)
    j = pl.program_id(1)
    nt = st_nt[s]
    t0 = st_t0[s]
    n_rows = nt * MOE_TM
    pieces = acc.shape[-1] // LANES
    groups_per_tile = MOE_TM // SUBLANES

    def table(ref, r):
        return ref[r // MOE_TM, (r % MOE_TM) // LANES, r % LANES]

    def slab_rows(ref, row):
        return ref.at[pl.ds(pl.multiple_of(row * pieces, pieces), pieces)]

    def row_in(r):
        return pltpu.make_async_copy(slab_rows(n2_hbm, table(tok_s, r)), slab_rows(slab, r), sem_g)

    def row_out(r):
        return pltpu.make_async_copy(slab_rows(slab, r), slab_rows(y_hbm, table(dst_s, r)), sem_s)

    @pl.when(nt > 0)
    def _active():
        @pl.when(j == 0)
        def _gather():
            c_tok = pltpu.make_async_copy(tok_hbm.at[pl.ds(t0, MOE_NTS)], tok_s, sem_i.at[0])
            c_dst = pltpu.make_async_copy(dst_hbm.at[pl.ds(t0, MOE_NTS)], dst_s, sem_i.at[1])
            c_tok.start()
            c_dst.start()
            c_tok.wait()
            c_dst.wait()

            def issue(r, c):
                row_in(r).start()
                return c

            def drain(r, c):
                row_in(r).wait()
                return c

            lax.fori_loop(0, n_rows, issue, 0)
            lax.fori_loop(0, n_rows, drain, 0)
            _from_slab(xs_b, slab, 0, n_rows // (2 * SUBLANES))

        wgb[...] = wg_ref[...].astype(BF16)
        wub[...] = wu_ref[...].astype(BF16)
        wdb[...] = wd_ref[...].astype(BF16)

        def tile(q, c):
            r0 = pl.multiple_of(q * MOE_TM, MOE_TM)
            xt = xs_b[pl.ds(r0, MOE_TM), :]
            g = jnp.dot(xt, wgb[...], preferred_element_type=F32) + bg_ref[...]
            u = jnp.dot(xt, wub[...], preferred_element_type=F32) + bu_ref[...]
            g = jnp.minimum(g, SWIGLU_LIMIT)
            u = jnp.clip(u, -SWIGLU_LIMIT, SWIGLU_LIMIT)
            hdn = g * jax.nn.sigmoid(SWIGLU_ALPHA * g) * (u + 1.0)
            part = jnp.dot(hdn.astype(BF16), wdb[...], preferred_element_type=F32)

            @pl.when(j == 0)
            def _():
                acc[pl.ds(r0, MOE_TM), :] = part + bd_ref[...]

            @pl.when(j > 0)
            def _():
                acc[pl.ds(r0, MOE_TM), :] = acc[pl.ds(r0, MOE_TM), :] + part

            return c

        lax.fori_loop(0, nt, tile, 0)

        @pl.when(j == n_j - 1)
        def _scatter():
            _to_slab(slab, acc, 0, nt * groups_per_tile)

            def issue(r, c):
                row_out(r).start()
                return c

            def drain(r, c):
                row_out(r).wait()
                return c

            lax.fori_loop(0, st_nv[s], issue, 0)
            lax.fori_loop(0, st_nv[s], drain, 0)


def _experts(n2_slab, tables, w_gate, b_gate, w_up, b_up, w_down, b_down):
    st_e, st_t0, st_nt, st_nv, row_token, row_dst = tables
    ne, d, f = w_gate.shape
    pieces = d // LANES
    y_rows = n2_slab.shape[0] // pieces * TOP_K
    n_j = f // MOE_TF
    s_max = st_e.shape[0]
    super_rows = MOE_NTS * MOE_TM

    def w_in_map(s, j, st_e, st_t0, st_nt, st_nv):
        return (st_e[s], 0, jnp.where(st_nt[s] > 0, j, n_j - 1))

    def w_down_map(s, j, st_e, st_t0, st_nt, st_nv):
        return (st_e[s], jnp.where(st_nt[s] > 0, j, n_j - 1), 0)

    def b_down_map(s, j, st_e, st_t0, st_nt, st_nv):
        return (st_e[s], 0, 0)

    any_spec = pl.BlockSpec(memory_space=pl.ANY)
    grid_spec = pltpu.PrefetchScalarGridSpec(
        num_scalar_prefetch=4,
        grid=(s_max, n_j),
        in_specs=[
            any_spec, any_spec, any_spec,
            pl.BlockSpec((None, d, MOE_TF), w_in_map),
            pl.BlockSpec((None, d, MOE_TF), w_in_map),
            pl.BlockSpec((None, MOE_TF, d), w_down_map),
            pl.BlockSpec((None, 1, MOE_TF), w_in_map),
            pl.BlockSpec((None, 1, MOE_TF), w_in_map),
            pl.BlockSpec((None, 1, d), b_down_map),
        ],
        out_specs=any_spec,
        scratch_shapes=[
            pltpu.SMEM((MOE_NTS, MOE_TM // LANES, LANES), I32),
            pltpu.SMEM((MOE_NTS, MOE_TM // LANES, LANES), I32),
            pltpu.VMEM((super_rows * pieces, LANES), F32),
            pltpu.VMEM((super_rows, d), BF16),
            pltpu.VMEM((super_rows, d), F32),
            pltpu.VMEM((d, MOE_TF), BF16),
            pltpu.VMEM((d, MOE_TF), BF16),
            pltpu.VMEM((MOE_TF, d), BF16),
            pltpu.SemaphoreType.DMA((2,)),
            pltpu.SemaphoreType.DMA(()),
            pltpu.SemaphoreType.DMA(()),
        ],
    )
    return pl.pallas_call(
        functools.partial(_moe_kernel, n_j=n_j),
        grid_spec=grid_spec,
        out_shape=jax.ShapeDtypeStruct((y_rows * pieces, LANES), F32),
        compiler_params=_cparams(("arbitrary", "arbitrary")),
        name="experts",
    )(st_e, st_t0, st_nt, st_nv, row_token, row_dst, n2_slab,
      w_gate, w_up, w_down, b_gate.reshape(ne, 1, f), b_up.reshape(ne, 1, f),
      b_down.reshape(ne, 1, d))


def _combine_kernel(h_ref, y0_ref, y1_ref, y2_ref, y3_ref, rf_ref, g_ref, o_ref, acc_ref, *, final_norm):
    tm, d = h_ref.shape
    pieces = d // LANES
    y_refs = (y0_ref, y1_ref, y2_ref, y3_ref)

    def group(g, c):
        r0 = pl.multiple_of(g * SUBLANES, SUBLANES)
        rf = rf_ref[pl.ds(r0, SUBLANES), :]
        gates = [jnp.broadcast_to(rf[:, k:k + 1], (SUBLANES, LANES)) for k in range(TOP_K)]
        for p in range(pieces):
            v = h_ref[pl.ds(r0, SUBLANES), p * LANES:(p + 1) * LANES]
            for k in range(TOP_K):
                v = v + gates[k] * y_refs[k][pl.ds(r0 * pieces + p, SUBLANES, stride=pieces), :]
            acc_ref[pl.ds(r0, SUBLANES), p * LANES:(p + 1) * LANES] = v
        return c

    lax.fori_loop(0, tm // SUBLANES, group, 0)
    o_ref[...] = _rms(acc_ref[...], g_ref[...]) if final_norm else acc_ref[...]


def _combine(h, y_slab, route_f, g, tm, final_norm):
    t, d = h.shape
    pieces = d // LANES
    y_spec = lambda k: pl.BlockSpec((tm * pieces, LANES), lambda i: (k * (t // tm) + i, 0))
    return pl.pallas_call(
        functools.partial(_combine_kernel, final_norm=final_norm),
        grid=(t // tm,),
        in_specs=[pl.BlockSpec((tm, d), lambda i: (i, 0)),
                  y_spec(0), y_spec(1), y_spec(2), y_spec(3),
                  pl.BlockSpec((tm, LANES), lambda i: (i, 0)),
                  pl.BlockSpec((1, d), lambda i: (0, 0))],
        out_specs=pl.BlockSpec((tm, d), lambda i: (i, 0)),
        out_shape=jax.ShapeDtypeStruct((t, d), F32),
        scratch_shapes=[pltpu.VMEM((tm, d), F32)],
        compiler_params=_cparams(("arbitrary",)),
        name="combine",
    )(h, y_slab, y_slab, y_slab, y_slab, route_f, g.reshape(1, d))


def kernel(x, norm1_g, w_in, rel_bias, ssm_lambda_re, ssm_lambda_im, ssm_log_step, ssm_b_re, ssm_b_im, ssm_c_re, ssm_c_im, ssm_d, w_ssm_glu, b_ssm_glu, attn_out_g, ssm_out_g, w_out, norm2_g, w_router, b_router, w_gate, b_gate, w_up, b_up, w_down, b_down, norm_f_g):
    b, l, d = x.shape
    depth = w_in.shape[0]
    aw = attn_out_g.shape[-1]
    n_heads = aw // HEAD_DIM
    n_experts = w_router.shape[-1]
    t = b * l
    assert l % ATT_TQ == 0 and l >= ATT_WIN and l % S5_TC == 0
    assert (t * TOP_K) % (MOE_NTS * MOE_TM) == 0

    h = x.reshape(t, d).astype(F32)
    for li in range(depth):
        last = li == depth - 1
        proj = _inproj(h, norm1_g[li], w_in[li].astype(BF16), tm=min(1024, t), tn=512)
        proj3 = proj.reshape(b, l, -1)
        mixed_a = _attention(proj3, _attn_bias_table(rel_bias[li]), attn_out_g[li], n_heads)
        wbd, cbd, avec = _s5_params(ssm_lambda_re[li], ssm_lambda_im[li], ssm_log_step[li],
                                    ssm_b_re[li], ssm_b_im[li], ssm_c_re[li], ssm_c_im[li])
        mixed_s = _s5(proj3, wbd, cbd, avec, ssm_d[li], w_ssm_glu[li].astype(BF16), b_ssm_glu[li],
                      ssm_out_g[li], u_block_index=3 * aw // (d - aw))
        h, n2_slab, route_i, route_f, cnt = _outproj(
            mixed_a.reshape(t, aw), mixed_s.reshape(t, d - aw), h, w_out[li].astype(BF16),
            norm2_g[li], w_router[li], b_router[li], tm=min(512, t))
        tables = _route_tables(route_i, cnt[0, :n_experts].astype(I32), t, n_experts)
        y_slab = _experts(n2_slab, tables, w_gate[li], b_gate[li], w_up[li], b_up[li], w_down[li],
                          b_down[li])
        g_fin = norm_f_g if last else jnp.ones((d,), F32)
        h = _combine(h, y_slab, route_f, g_fin, tm=min(256, t), final_norm=last)
    return h.reshape(b, l, d).astype(x.dtype)
```

```python
import functools
import math

import jax
import jax.numpy as jnp
from jax import lax
from jax.experimental import pallas as pl
from jax.experimental.pallas import tpu as pltpu

F32 = jnp.float32
BF16 = jnp.bfloat16
I32 = jnp.int32

CHUNK = 64
HEAD_DIM = 128
N_BACK_CHUNKS = 8
REL_CLIP = 128
SSM_GROUP = 16
SSM_STATE = 64
TOP_K = 4
SWIGLU_LIMIT = 7.0
SWIGLU_ALPHA = 1.702
EPS = 1e-5
NEG_BIG = -1e30

LANES = 128
SUBLANES = 8
MXU_DIM = 256
VMEM_LIMIT_BYTES = 56 * 1024 * 1024

ATT_TQ = 2 * CHUNK
ATT_WIN = (N_BACK_CHUNKS + 2) * CHUNK
ATT_NSHIFT = (N_BACK_CHUNKS * CHUNK) // ATT_TQ + 1

S5_GROUPS_PER_BLOCK = MXU_DIM // SSM_GROUP
S5_BLOCK_STATES = S5_GROUPS_PER_BLOCK * SSM_STATE
S5_STRIP = 512
S5_TC = 32

MOE_TM = 256
MOE_NTS = 4
MOE_TF = 512
DISPATCH_TM = 512
COMBINE_TM = 256


def _cparams(sem):
    return pltpu.CompilerParams(dimension_semantics=sem, vmem_limit_bytes=VMEM_LIMIT_BYTES)


def _rms(x, g):
    r = lax.rsqrt(jnp.mean(x * x, axis=-1, keepdims=True) + EPS)
    return x * r * g


def _to_slab(dst_ref, src_ref, n_groups):
    pieces = src_ref.shape[-1] // LANES

    def group(g, c):
        r0 = pl.multiple_of(g * SUBLANES, SUBLANES)
        for p in range(pieces):
            dst_ref[pl.ds(r0 * pieces + p, SUBLANES, stride=pieces), :] = (
                src_ref[pl.ds(r0, SUBLANES), p * LANES:(p + 1) * LANES])
        return c

    lax.fori_loop(0, n_groups, group, 0)


def _from_slab(dst_ref, slab_ref, n_groups):
    pieces = dst_ref.shape[-1] // LANES
    rows = 2 * SUBLANES

    def group(g, c):
        r0 = pl.multiple_of(g * rows, rows)
        for p in range(pieces):
            lo = slab_ref[pl.ds(r0 * pieces + p, SUBLANES, stride=pieces), :]
            hi = slab_ref[pl.ds((r0 + SUBLANES) * pieces + p, SUBLANES, stride=pieces), :]
            dst_ref[pl.ds(r0, rows), p * LANES:(p + 1) * LANES] = (
                jnp.concatenate([lo, hi], axis=0).astype(dst_ref.dtype))
        return c

    lax.fori_loop(0, n_groups, group, 0)


def _inproj_kernel(x_ref, g_ref, w_ref, o_ref, nb_ref):
    @pl.when(pl.program_id(1) == 0)
    def _():
        nb_ref[...] = _rms(x_ref[...], g_ref[...]).astype(BF16)

    o_ref[...] = jnp.dot(nb_ref[...], w_ref[...], preferred_element_type=F32).astype(o_ref.dtype)


def _inproj(x2d, g, w_bf16, tm, tn):
    t, d = x2d.shape
    n = w_bf16.shape[1]
    return pl.pallas_call(
        _inproj_kernel,
        grid=(t // tm, n // tn),
        in_specs=[
            pl.BlockSpec((tm, d), lambda i, j: (i, 0)),
            pl.BlockSpec((1, d), lambda i, j: (0, 0)),
            pl.BlockSpec((d, tn), lambda i, j: (0, j)),
        ],
        out_specs=pl.BlockSpec((tm, tn), lambda i, j: (i, j)),
        out_shape=jax.ShapeDtypeStruct((t, n), BF16),
        scratch_shapes=[pltpu.VMEM((tm, d), BF16)],
        compiler_params=_cparams(("arbitrary", "arbitrary")),
        name="inproj",
    )(x2d, g.reshape(1, d), w_bf16)


def _attn_bias_table(rel_bias):
    n_heads = rel_bias.shape[0]
    period = ATT_WIN + ATT_TQ
    m = jnp.arange(period)
    diff = jnp.where(m < ATT_WIN, m, m - period)
    sh = jnp.arange(ATT_NSHIFT)[:, None]
    rel = N_BACK_CHUNKS * CHUNK - ATT_TQ * sh - diff[None, :]
    idx = jnp.clip(rel, -REL_CLIP, REL_CLIP) + REL_CLIP
    vext = rel_bias.astype(F32)[:, idx]
    flat = jnp.tile(vext, (1, 1, ATT_TQ))[..., :ATT_TQ * (period - 1)]
    tbl = flat.reshape(n_heads, ATT_NSHIFT, ATT_TQ, period - 1)[..., :ATT_WIN]
    return jnp.transpose(tbl, (1, 0, 2, 3))


def _attn_kernel(q_ref, k_ref, v_ref, b_ref, g_ref, o_ref, a_ref, *, n_heads):
    i = pl.program_id(1)
    s0 = pl.multiple_of(jnp.maximum(i * ATT_TQ - N_BACK_CHUNKS * CHUNK, 0), ATT_TQ)
    row = lax.broadcasted_iota(I32, (ATT_TQ, ATT_WIN), 0)
    col = lax.broadcasted_iota(I32, (ATT_TQ, ATT_WIN), 1)
    qc = (i * ATT_TQ + row) // CHUNK
    kc = (s0 + col) // CHUNK
    ok = (kc <= qc) & (kc >= qc - N_BACK_CHUNKS)
    scale = 1.0 / math.sqrt(HEAD_DIM)
    for h in range(n_heads):
        hs = slice(h * HEAD_DIM, (h + 1) * HEAD_DIM)
        qh = q_ref[:, hs]
        kh = k_ref[pl.ds(s0, ATT_WIN), hs]
        vh = v_ref[pl.ds(s0, ATT_WIN), hs]
        s = lax.dot_general(qh, kh, (((1,), (1,)), ((), ())), preferred_element_type=F32)
        s = jnp.where(ok, s * scale + b_ref[h], NEG_BIG)
        m = jnp.max(s, axis=-1, keepdims=True)
        p = jnp.exp(s - m)
        l = jnp.sum(p, axis=-1, keepdims=True)
        o = jnp.dot(p.astype(BF16), vh, preferred_element_type=F32)
        a_ref[:, hs] = o / l
    o_ref[...] = _rms(a_ref[...], g_ref[...]).astype(o_ref.dtype)


def _attention(proj3, bias_tbl, g, n_heads):
    b, l, _ = proj3.shape
    aw = n_heads * HEAD_DIM
    nsh = ATT_NSHIFT
    return pl.pallas_call(
        functools.partial(_attn_kernel, n_heads=n_heads),
        grid=(b, l // ATT_TQ),
        in_specs=[
            pl.BlockSpec((None, ATT_TQ, aw), lambda bi, i: (bi, i, 0)),
            pl.BlockSpec((None, l, aw), lambda bi, i: (bi, 0, 1)),
            pl.BlockSpec((None, l, aw), lambda bi, i: (bi, 0, 2)),
            pl.BlockSpec((None, n_heads, ATT_TQ, ATT_WIN),
                         lambda bi, i: (jnp.maximum(nsh - 1 - i, 0), 0, 0, 0)),
            pl.BlockSpec((1, aw), lambda bi, i: (0, 0)),
        ],
        out_specs=pl.BlockSpec((None, ATT_TQ, aw), lambda bi, i: (bi, i, 0)),
        out_shape=jax.ShapeDtypeStruct((b, l, aw), BF16),
        scratch_shapes=[pltpu.VMEM((ATT_TQ, aw), F32)],
        compiler_params=_cparams(("arbitrary", "arbitrary")),
        name="attn",
    )(proj3, proj3, proj3, bias_tbl, g.reshape(1, aw))


def _s5_params(lam_re, lam_im, log_step, b_re, b_im, c_re, c_im):
    g, p = lam_re.shape
    nb = g // S5_GROUPS_PER_BLOCK
    gl = S5_GROUPS_PER_BLOCK
    dt = jnp.exp(log_step.astype(F32))[:, None]
    lr = lam_re.astype(F32)
    li = lam_im.astype(F32)
    mag = jnp.exp(lr * dt)
    ab_re = mag * jnp.cos(li * dt)
    ab_im = mag * jnp.sin(li * dt)
    den = lr * lr + li * li
    nr = ab_re - 1.0
    ni = ab_im
    f_re = (nr * lr + ni * li) / den
    f_im = (ni * lr - nr * li) / den
    br = b_re.astype(F32)
    bi = b_im.astype(F32)
    bb_re = f_re[..., None] * br - f_im[..., None] * bi
    bb_im = f_re[..., None] * bi + f_im[..., None] * br
    eye = jnp.eye(gl, dtype=F32)
    bb = jnp.stack([bb_re, bb_im]).reshape(2, nb, gl, p, SSM_GROUP)
    wbd = jnp.einsum("ajgpc,gh->jgcahp", bb, eye).reshape(nb, gl * SSM_GROUP, 2 * gl * p)
    cc = jnp.stack([c_re.astype(F32), -c_im.astype(F32)]).reshape(2, nb, gl, SSM_GROUP, p)
    cbd = jnp.einsum("ajgcp,gh->jagphc", cc, eye).reshape(nb, 2 * gl * p, gl * SSM_GROUP)
    avec = jnp.stack([ab_re, ab_im]).reshape(2, nb, gl, p)
    avec = jnp.transpose(avec, (1, 0, 2, 3)).reshape(1, 2 * g * p)
    return wbd.astype(BF16), cbd.astype(BF16), avec


def _s5_kernel(u_ref, pm_ref, pmt_ref, wbd_ref, cbd_ref, av_ref, d_ref, wglu_ref, bglu_ref,
               g_ref, o_ref, xs_ref, hst_ref, *, n_blocks, tc, batch):
    rows = tc * batch
    sw = u_ref.shape[-1]
    bw = 2 * S5_BLOCK_STATES
    cw = S5_GROUPS_PER_BLOCK * SSM_GROUP

    @pl.when(pl.program_id(0) == 0)
    def _():
        hst_ref[...] = jnp.zeros_like(hst_ref)

    u_tm = jnp.dot(pm_ref[...], u_ref[...].reshape(rows, sw), preferred_element_type=F32)
    u_tmb = u_tm.astype(BF16)
    for j in range(n_blocks):
        xs_ref[:, j * bw:(j + 1) * bw] = jnp.dot(
            u_tmb[:, j * cw:(j + 1) * cw], wbd_ref[j], preferred_element_type=F32)

    for j in range(n_blocks):
        for s in range(S5_BLOCK_STATES // S5_STRIP):
            cr = j * bw + s * S5_STRIP
            ci = cr + S5_BLOCK_STATES
            a_re = jnp.broadcast_to(av_ref[:, cr:cr + S5_STRIP], (batch, S5_STRIP))
            a_im = jnp.broadcast_to(av_ref[:, ci:ci + S5_STRIP], (batch, S5_STRIP))

            def step(t, carry, cr=cr, ci=ci, a_re=a_re, a_im=a_im):
                h_re, h_im = carry
                r0 = pl.multiple_of(t * batch, batch)
                n_re = a_re * h_re - a_im * h_im + xs_ref[pl.ds(r0, batch), cr:cr + S5_STRIP]
                n_im = a_re * h_im + a_im * h_re + xs_ref[pl.ds(r0, batch), ci:ci + S5_STRIP]
                xs_ref[pl.ds(r0, batch), cr:cr + S5_STRIP] = n_re
                xs_ref[pl.ds(r0, batch), ci:ci + S5_STRIP] = n_im
                return n_re, n_im

            h_re, h_im = lax.fori_loop(
                0, tc, step, (hst_ref[:, cr:cr + S5_STRIP], hst_ref[:, ci:ci + S5_STRIP]), unroll=4)
            hst_ref[:, cr:cr + S5_STRIP] = h_re
            hst_ref[:, ci:ci + S5_STRIP] = h_im

    ys = [jnp.dot(xs_ref[:, j * bw:(j + 1) * bw].astype(BF16), cbd_ref[j], preferred_element_type=F32)
          for j in range(n_blocks)]
    y = jnp.concatenate(ys, axis=1) + d_ref[...] * u_tm
    y = jax.nn.gelu(y)
    z = jnp.dot(y.astype(BF16), wglu_ref[...], preferred_element_type=F32) + bglu_ref[...]
    y = y * jax.nn.sigmoid(z)
    res = _rms(y, g_ref[...]).astype(BF16)
    out_bm = jnp.dot(pmt_ref[...], res, preferred_element_type=F32).astype(o_ref.dtype)
    o_ref[...] = out_bm.reshape(batch, tc, sw)


def _s5(proj3, wbd, cbd, avec, d, wglu_bf16, bglu, g, u_block_index):
    b, l, _ = proj3.shape
    assert b == SUBLANES, "the scan keeps the batch on the sublane axis"
    n_blocks, cw, bw = wbd.shape
    sw = n_blocks * cw
    tc = S5_TC
    rows = tc * b
    r = jnp.arange(rows)
    src = (r % b) * tc + r // b
    pm = (src[:, None] == jnp.arange(rows)[None, :]).astype(BF16)
    pmt = pm.T
    const = lambda shape: pl.BlockSpec(shape, lambda c: (0,) * len(shape))
    return pl.pallas_call(
        functools.partial(_s5_kernel, n_blocks=n_blocks, tc=tc, batch=b),
        grid=(l // tc,),
        in_specs=[
            pl.BlockSpec((b, tc, sw), lambda c: (0, c, u_block_index)),
            const((rows, rows)), const((rows, rows)),
            const((n_blocks, cw, bw)), const((n_blocks, bw, cw)),
            const((1, n_blocks * bw)), const((1, sw)),
            const((sw, sw)), const((1, sw)), const((1, sw)),
        ],
        out_specs=pl.BlockSpec((b, tc, sw), lambda c: (0, c, 0)),
        out_shape=jax.ShapeDtypeStruct((b, l, sw), BF16),
        scratch_shapes=[pltpu.VMEM((rows, n_blocks * bw), F32),
                        pltpu.VMEM((b, n_blocks * bw), F32)],
        compiler_params=_cparams(("arbitrary",)),
        name="s5",
    )(proj3, pm, pmt, wbd, cbd, avec, d.reshape(1, sw), wglu_bf16, bglu.reshape(1, sw),
      g.reshape(1, sw))


def _outproj_kernel(ma_ref, ms_ref, x_ref, wo_ref, g2_ref, wr_ref, br_ref, tri_ref,
                    h_ref, n2s_ref, ri_ref, rf_ref, cnt_ref, carry_ref, n2_ref):
    aw = ma_ref.shape[-1]

    @pl.when(pl.program_id(0) == 0)
    def _():
        carry_ref[...] = jnp.zeros_like(carry_ref)

    h = (x_ref[...]
         + jnp.dot(ma_ref[...], wo_ref[:aw, :], preferred_element_type=F32)
         + jnp.dot(ms_ref[...], wo_ref[aw:, :], preferred_element_type=F32))
    h_ref[...] = h
    n2 = _rms(h, g2_ref[...])
    n2_ref[...] = n2
    _to_slab(n2s_ref, n2_ref, n2.shape[0] // SUBLANES)

    n_hi = n2.astype(BF16)
    n_lo = (n2 - n_hi.astype(F32)).astype(BF16)
    w = wr_ref[...]
    w_hi = w.astype(BF16)
    w_lo = (w - w_hi.astype(F32)).astype(BF16)
    logits = (jnp.dot(n_hi, w_hi, preferred_element_type=F32)
              + jnp.dot(n_lo, w_hi, preferred_element_type=F32)
              + jnp.dot(n_hi, w_lo, preferred_element_type=F32)) + br_ref[...]

    tm, ne = logits.shape
    lane = lax.broadcasted_iota(I32, (tm, ne), 1)
    work = logits
    vals, idxs = [], []
    for _ in range(TOP_K):
        m = jnp.max(work, axis=-1, keepdims=True)
        ik = jnp.min(jnp.where(work == m, lane, ne), axis=-1, keepdims=True)
        vals.append(m)
        idxs.append(ik)
        work = jnp.where(lane == ik, -jnp.inf, work)
    es = [jnp.exp(v - vals[0]) for v in vals]
    den = es[0] + es[1] + es[2] + es[3]

    hot = jnp.zeros((tm, ne), F32)
    for ik in idxs:
        hot = hot + (lane == ik).astype(F32)
    before = jnp.dot(tri_ref[...], hot.astype(BF16), preferred_element_type=F32) + carry_ref[...]
    ri = jnp.zeros((tm, ne), I32)
    rf = jnp.zeros((tm, ne), F32)
    for k in range(TOP_K):
        rank = jnp.sum(jnp.where(lane == idxs[k], before, 0.0), axis=-1, keepdims=True)
        ri = jnp.where(lane == k, idxs[k], ri)
        ri = jnp.where(lane == TOP_K + k, rank.astype(I32), ri)
        rf = jnp.where(lane == k, es[k] / den, rf)
    ri_ref[...] = ri
    rf_ref[...] = rf
    carry_ref[...] = carry_ref[...] + jnp.sum(hot, axis=0, keepdims=True)
    cnt_ref[...] = carry_ref[...]


def _outproj(ma, ms, x2d, wo_bf16, g2, w_router, b_router, tm):
    t, d = x2d.shape
    aw = ma.shape[1]
    ne = w_router.shape[1]
    pieces = d // LANES
    wr = jnp.zeros((d, LANES), F32).at[:, :ne].set(w_router.astype(F32))
    br = jnp.full((1, LANES), NEG_BIG, F32).at[0, :ne].set(b_router.astype(F32))
    tri = (jnp.arange(tm)[:, None] > jnp.arange(tm)[None, :]).astype(BF16)
    const = lambda shape: pl.BlockSpec(shape, lambda i: (0,) * len(shape))
    row = lambda w: pl.BlockSpec((tm, w), lambda i: (i, 0))
    return pl.pallas_call(
        _outproj_kernel,
        grid=(t // tm,),
        in_specs=[row(aw), row(ms.shape[1]), row(d), const((d, d)), const((1, d)),
                  const((d, LANES)), const((1, LANES)), const((tm, tm))],
        out_specs=[row(d), pl.BlockSpec((tm * pieces, LANES), lambda i: (i, 0)),
                   row(LANES), row(LANES), const((1, LANES))],
        out_shape=[jax.ShapeDtypeStruct((t, d), F32), jax.ShapeDtypeStruct((t * pieces, LANES), F32),
                   jax.ShapeDtypeStruct((t, LANES), I32), jax.ShapeDtypeStruct((t, LANES), F32),
                   jax.ShapeDtypeStruct((1, LANES), F32)],
        scratch_shapes=[pltpu.VMEM((1, LANES), F32), pltpu.VMEM((tm, d), F32)],
        compiler_params=_cparams(("arbitrary",)),
        name="outproj",
    )(ma, ms, x2d, wo_bf16, g2.reshape(1, d), wr, br, tri)


def _route_tables(route_i, counts, n_tokens, n_experts):
    idx = route_i[:, :TOP_K]
    rank = route_i[:, TOP_K:2 * TOP_K]
    n_tiles_max = n_tokens * TOP_K // MOE_TM + n_experts
    s_max = n_experts + n_tiles_max // MOE_NTS
    e_iota = jnp.arange(n_experts, dtype=I32)

    tiles_e = (counts + MOE_TM - 1) // MOE_TM
    tile_end = jnp.cumsum(tiles_e)
    tile_off = tile_end - tiles_e
    off_of = jnp.sum(jnp.where(idx[..., None] == e_iota, tile_off, 0), axis=-1)
    pos = (off_of * MOE_TM + rank).astype(I32)

    nst_e = (tiles_e + MOE_NTS - 1) // MOE_NTS
    st_incl = jnp.cumsum(nst_e)
    num_st = st_incl[-1]
    s = jnp.arange(s_max, dtype=I32)
    active = s < num_st
    s_eff = jnp.minimum(s, num_st - 1)
    e_s = jnp.minimum(jnp.sum(st_incl[None, :] <= s_eff[:, None], axis=1), n_experts - 1).astype(I32)
    onehot = e_s[:, None] == e_iota[None, :]
    pick = lambda v: jnp.sum(jnp.where(onehot, v[None, :], 0), axis=1)
    local = s_eff - (pick(st_incl) - pick(nst_e))
    tile0 = (pick(tile_off) + MOE_NTS * local).astype(I32)
    nt = jnp.where(active, jnp.minimum(MOE_NTS, pick(tiles_e) - MOE_NTS * local), 0).astype(I32)

    tile_ids = jnp.arange(n_tiles_max, dtype=I32)
    is_last = jnp.any((tile_ids[:, None] == tile_end[None, :] - 1) & (tiles_e[None, :] > 0), axis=1)
    pad_tile = (is_last | (tile_ids >= tile_end[-1])).astype(I32)
    return pos, e_s, tile0, nt, pad_tile, tile_end[-1:].astype(I32)


def _dispatch_kernel(pad_tile, pos_ref, n2_hbm, xs_hbm, zero_ref, sem_z, sem_r, *,
                     tokens_per_step, pieces):
    i = pl.program_id(0)
    tile_rows = MOE_TM * pieces
    n_tiles = pad_tile.shape[0]

    @pl.when(i == 0)
    def _zero_fill():
        zero_ref[...] = jnp.zeros_like(zero_ref)

        def fill(tl):
            start = pl.multiple_of(tl * tile_rows, tile_rows)
            return pltpu.make_async_copy(zero_ref, xs_hbm.at[pl.ds(start, tile_rows)], sem_z)

        def issue(tl, c):
            @pl.when(pad_tile[tl] > 0)
            def _():
                fill(tl).start()
            return c

        def drain(tl, c):
            @pl.when(pad_tile[tl] > 0)
            def _():
                fill(tl).wait()
            return c

        lax.fori_loop(0, n_tiles, issue, 0)
        lax.fori_loop(0, n_tiles, drain, 0)

    def row_copy(tok, dst):
        return pltpu.make_async_copy(
            n2_hbm.at[pl.ds(pl.multiple_of(tok * pieces, pieces), pieces)],
            xs_hbm.at[pl.ds(pl.multiple_of(dst * pieces, pieces), pieces)], sem_r)

    tok0 = i * tokens_per_step

    def table_row(m, c):
        def one(b, c2):
            a = m * LANES + b
            row_copy(tok0 + lax.shift_right_logical(a, 2), pos_ref[m, b]).start()
            return c2

        lax.fori_loop(0, LANES, one, 0, unroll=8)
        return c

    n_table_rows = tokens_per_step * TOP_K // LANES
    lax.fori_loop(0, n_table_rows, table_row, 0)

    def drain_table_row(m, c):
        for _ in range(LANES):
            row_copy(0, 0).wait()
        return c

    lax.fori_loop(0, n_table_rows, drain_table_row, 0)


def _dispatch(n2_slab, pos, pad_tile, d):
    pieces = d // LANES
    n_tokens = n2_slab.shape[0] // pieces
    n_tiles = pad_tile.shape[0]
    tm = min(DISPATCH_TM, n_tokens)
    table_rows = tm * TOP_K // LANES
    grid_spec = pltpu.PrefetchScalarGridSpec(
        num_scalar_prefetch=1,
        grid=(n_tokens // tm,),
        in_specs=[pl.BlockSpec((table_rows, LANES), lambda i, pad: (i, 0), memory_space=pltpu.SMEM),
                  pl.BlockSpec(memory_space=pl.ANY)],
        out_specs=pl.BlockSpec(memory_space=pl.ANY),
        scratch_shapes=[pltpu.VMEM((MOE_TM * pieces, LANES), F32),
                        pltpu.SemaphoreType.DMA(()), pltpu.SemaphoreType.DMA(())],
    )
    return pl.pallas_call(
        functools.partial(_dispatch_kernel, tokens_per_step=tm, pieces=pieces),
        grid_spec=grid_spec,
        out_shape=jax.ShapeDtypeStruct((n_tiles * MOE_TM * pieces, LANES), F32),
        compiler_params=_cparams(("arbitrary",)),
        name="dispatch",
    )(pad_tile, pos.reshape(-1, LANES), n2_slab)


def _experts_kernel(st_e, st_t0, st_nt, n_used, xs_hbm, wg_ref, wu_ref, wd_ref, bg_ref, bu_ref, bd_ref,
                    y_hbm, slab, xs_b, acc, wgb, wub, wdb, sem_g, sem_s, *, n_j, n_tiles):
    s = pl.program_id(0)
    j = pl.program_id(1)
    nt = st_nt[s]
    t0 = st_t0[s]
    pieces = acc.shape[-1] // LANES
    tile_rows = MOE_TM * pieces
    groups_per_tile = MOE_TM // SUBLANES

    def tile_in(q):
        return pltpu.make_async_copy(
            xs_hbm.at[pl.ds(pl.multiple_of((t0 + q) * tile_rows, tile_rows), tile_rows)],
            slab.at[pl.ds(pl.multiple_of(q * tile_rows, tile_rows), tile_rows)], sem_g)

    def tile_out(q, tl):
        return pltpu.make_async_copy(
            slab.at[pl.ds(pl.multiple_of(q * tile_rows, tile_rows), tile_rows)],
            y_hbm.at[pl.ds(pl.multiple_of(tl * tile_rows, tile_rows), tile_rows)], sem_s)

    def for_tiles(lo, hi, fn):
        def body(q, c):
            fn(q)
            return c
        lax.fori_loop(lo, hi, body, 0)

    @pl.when((s == 0) & (j == 0))
    def _zero_tail():
        slab[pl.ds(0, tile_rows), :] = jnp.zeros((tile_rows, LANES), F32)
        for_tiles(n_used[0], n_tiles, lambda tl: tile_out(0, tl).start())
        for_tiles(n_used[0], n_tiles, lambda tl: tile_out(0, tl).wait())

    @pl.when(nt > 0)
    def _active():
        @pl.when(j == 0)
        def _load():
            for_tiles(0, nt, lambda q: tile_in(q).start())
            for_tiles(0, nt, lambda q: tile_in(q).wait())
            _from_slab(xs_b, slab, nt * (groups_per_tile // 2))

        wgb[...] = wg_ref[...].astype(BF16)
        wub[...] = wu_ref[...].astype(BF16)
        wdb[...] = wd_ref[...].astype(BF16)

        def tile(q, c):
            r0 = pl.multiple_of(q * MOE_TM, MOE_TM)
            xt = xs_b[pl.ds(r0, MOE_TM), :]
            g = jnp.dot(xt, wgb[...], preferred_element_type=F32) + bg_ref[...]
            u = jnp.dot(xt, wub[...], preferred_element_type=F32) + bu_ref[...]
            g = jnp.minimum(g, SWIGLU_LIMIT)
            u = jnp.clip(u, -SWIGLU_LIMIT, SWIGLU_LIMIT)
            hdn = g * jax.nn.sigmoid(SWIGLU_ALPHA * g) * (u + 1.0)
            part = jnp.dot(hdn.astype(BF16), wdb[...], preferred_element_type=F32)

            @pl.when(j == 0)
            def _():
                acc[pl.ds(r0, MOE_TM), :] = part + bd_ref[...]

            @pl.when(j > 0)
            def _():
                acc[pl.ds(r0, MOE_TM), :] = acc[pl.ds(r0, MOE_TM), :] + part

            return c

        lax.fori_loop(0, nt, tile, 0)

        @pl.when(j == n_j - 1)
        def _store():
            _to_slab(slab, acc, nt * groups_per_tile)
            for_tiles(0, nt, lambda q: tile_out(q, t0 + q).start())
            for_tiles(0, nt, lambda q: tile_out(q, t0 + q).wait())


def _experts(xs_slab, tables, w_gate, b_gate, w_up, b_up, w_down, b_down):
    st_e, st_t0, st_nt, n_used = tables
    ne, d, f = w_gate.shape
    pieces = d // LANES
    n_tiles = xs_slab.shape[0] // (MOE_TM * pieces)
    n_j = f // MOE_TF
    s_max = st_e.shape[0]
    super_rows = MOE_NTS * MOE_TM

    def w_in_map(s, j, st_e, st_t0, st_nt, n_used):
        return (st_e[s], 0, jnp.where(st_nt[s] > 0, j, n_j - 1))

    def w_down_map(s, j, st_e, st_t0, st_nt, n_used):
        return (st_e[s], jnp.where(st_nt[s] > 0, j, n_j - 1), 0)

    def b_down_map(s, j, st_e, st_t0, st_nt, n_used):
        return (st_e[s], 0, 0)

    any_spec = pl.BlockSpec(memory_space=pl.ANY)
    grid_spec = pltpu.PrefetchScalarGridSpec(
        num_scalar_prefetch=4,
        grid=(s_max, n_j),
        in_specs=[
            any_spec,
            pl.BlockSpec((None, d, MOE_TF), w_in_map),
            pl.BlockSpec((None, d, MOE_TF), w_in_map),
            pl.BlockSpec((None, MOE_TF, d), w_down_map),
            pl.BlockSpec((None, 1, MOE_TF), w_in_map),
            pl.BlockSpec((None, 1, MOE_TF), w_in_map),
            pl.BlockSpec((None, 1, d), b_down_map),
        ],
        out_specs=any_spec,
        scratch_shapes=[
            pltpu.VMEM((super_rows * pieces, LANES), F32),
            pltpu.VMEM((super_rows, d), BF16),
            pltpu.VMEM((super_rows, d), F32),
            pltpu.VMEM((d, MOE_TF), BF16),
            pltpu.VMEM((d, MOE_TF), BF16),
            pltpu.VMEM((MOE_TF, d), BF16),
            pltpu.SemaphoreType.DMA(()),
            pltpu.SemaphoreType.DMA(()),
        ],
    )
    return pl.pallas_call(
        functools.partial(_experts_kernel, n_j=n_j, n_tiles=n_tiles),
        grid_spec=grid_spec,
        out_shape=jax.ShapeDtypeStruct(xs_slab.shape, F32),
        compiler_params=_cparams(("arbitrary", "arbitrary")),
        name="experts",
    )(st_e, st_t0, st_nt, n_used, xs_slab,
      w_gate, w_up, w_down, b_gate.reshape(ne, 1, f), b_up.reshape(ne, 1, f),
      b_down.reshape(ne, 1, d))


def _combine_kernel(pos_ref, h_ref, rf_ref, g_ref, y_hbm, o_ref, yv, acc_ref, sem, *, final_norm):
    tm, d = h_ref.shape
    pieces = d // LANES

    def row_copy(src, dst):
        return pltpu.make_async_copy(
            y_hbm.at[pl.ds(pl.multiple_of(src * pieces, pieces), pieces)],
            yv.at[pl.ds(pl.multiple_of(dst * pieces, pieces), pieces)], sem)

    def table_row(m, c):
        def one(b, c2):
            a = m * LANES + b
            row_copy(pos_ref[m, b], (a & (TOP_K - 1)) * tm + lax.shift_right_logical(a, 2)).start()
            return c2

        lax.fori_loop(0, LANES, one, 0, unroll=8)
        return c

    lax.fori_loop(0, tm * TOP_K // LANES, table_row, 0)

    def drain_table_row(m, c):
        for _ in range(LANES):
            row_copy(0, 0).wait()
        return c

    lax.fori_loop(0, tm * TOP_K // LANES, drain_table_row, 0)

    def group(g, c):
        r0 = pl.multiple_of(g * SUBLANES, SUBLANES)
        rf = rf_ref[pl.ds(r0, SUBLANES), :]
        gates = [jnp.broadcast_to(rf[:, k:k + 1], (SUBLANES, LANES)) for k in range(TOP_K)]
        for p in range(pieces):
            v = h_ref[pl.ds(r0, SUBLANES), p * LANES:(p + 1) * LANES]
            for k in range(TOP_K):
                v = v + gates[k] * yv[pl.ds((k * tm + r0) * pieces + p, SUBLANES, stride=pieces), :]
            acc_ref[pl.ds(r0, SUBLANES), p * LANES:(p + 1) * LANES] = v
        return c

    lax.fori_loop(0, tm // SUBLANES, group, 0)
    o_ref[...] = _rms(acc_ref[...], g_ref[...]) if final_norm else acc_ref[...]


def _combine(h, y_slab, pos, route_f, g, final_norm):
    t, d = h.shape
    pieces = d // LANES
    tm = min(COMBINE_TM, t)
    table_rows = tm * TOP_K // LANES
    return pl.pallas_call(
        functools.partial(_combine_kernel, final_norm=final_norm),
        grid=(t // tm,),
        in_specs=[pl.BlockSpec((table_rows, LANES), lambda i: (i, 0), memory_space=pltpu.SMEM),
                  pl.BlockSpec((tm, d), lambda i: (i, 0)),
                  pl.BlockSpec((tm, LANES), lambda i: (i, 0)),
                  pl.BlockSpec((1, d), lambda i: (0, 0)),
                  pl.BlockSpec(memory_space=pl.ANY)],
        out_specs=pl.BlockSpec((tm, d), lambda i: (i, 0)),
        out_shape=jax.ShapeDtypeStruct((t, d), F32),
        scratch_shapes=[pltpu.VMEM((TOP_K * tm * pieces, LANES), F32), pltpu.VMEM((tm, d), F32),
                        pltpu.SemaphoreType.DMA(())],
        compiler_params=_cparams(("arbitrary",)),
        name="combine",
    )(pos.reshape(-1, LANES), h, route_f, g.reshape(1, d), y_slab)


def kernel(x, norm1_g, w_in, rel_bias, ssm_lambda_re, ssm_lambda_im, ssm_log_step, ssm_b_re, ssm_b_im, ssm_c_re, ssm_c_im, ssm_d, w_ssm_glu, b_ssm_glu, attn_out_g, ssm_out_g, w_out, norm2_g, w_router, b_router, w_gate, b_gate, w_up, b_up, w_down, b_down, norm_f_g):
    b, l, d = x.shape
    depth = w_in.shape[0]
    aw = attn_out_g.shape[-1]
    n_heads = aw // HEAD_DIM
    n_experts = w_router.shape[-1]
    t = b * l
    assert l % ATT_TQ == 0 and l >= ATT_WIN and l % S5_TC == 0
    assert (t * TOP_K) % (MOE_NTS * MOE_TM) == 0 and t % DISPATCH_TM == 0

    h = x.reshape(t, d).astype(F32)
    for li in range(depth):
        last = li == depth - 1
        proj = _inproj(h, norm1_g[li], w_in[li].astype(BF16), tm=min(1024, t), tn=512)
        proj3 = proj.reshape(b, l, -1)
        mixed_a = _attention(proj3, _attn_bias_table(rel_bias[li]), attn_out_g[li], n_heads)
        wbd, cbd, avec = _s5_params(ssm_lambda_re[li], ssm_lambda_im[li], ssm_log_step[li],
                                    ssm_b_re[li], ssm_b_im[li], ssm_c_re[li], ssm_c_im[li])
        mixed_s = _s5(proj3, wbd, cbd, avec, ssm_d[li], w_ssm_glu[li].astype(BF16), b_ssm_glu[li],
                      ssm_out_g[li], u_block_index=3 * aw // (d - aw))
        h, n2_slab, route_i, route_f, cnt = _outproj(
            mixed_a.reshape(t, aw), mixed_s.reshape(t, d - aw), h, w_out[li].astype(BF16),
            norm2_g[li], w_router[li], b_router[li], tm=min(512, t))
        pos, st_e, st_t0, st_nt, pad_tile, n_used = _route_tables(
            route_i, cnt[0, :n_experts].astype(I32), t, n_experts)
        xs_slab = _dispatch(n2_slab, pos, pad_tile, d)
        ys_slab = _experts(xs_slab, (st_e, st_t0, st_nt, n_used), w_gate[li], b_gate[li], w_up[li],
                           b_up[li], w_down[li], b_down[li])
        g_fin = norm_f_g if last else jnp.ones((d,), F32)
        h = _combine(h, ys_slab, pos, route_f, g_fin, final_norm=last)
    return h.reshape(b, l, d).astype(x.dtype)
```

```python
import functools
import math

import jax
import jax.numpy as jnp
from jax import lax
from jax.experimental import pallas as pl
from jax.experimental.pallas import tpu as pltpu

F32 = jnp.float32
BF16 = jnp.bfloat16
I32 = jnp.int32

CHUNK = 64
HEAD_DIM = 128
N_BACK_CHUNKS = 8
REL_CLIP = 128
SSM_GROUP = 16
SSM_STATE = 64
TOP_K = 4
SWIGLU_LIMIT = 7.0
SWIGLU_ALPHA = 1.702
EPS = 1e-5
NEG_BIG = -1e30

LANES = 128
SUBLANES = 8
MXU_DIM = 256
VMEM_LIMIT_BYTES = 56 * 1024 * 1024

ATT_TQ = 2 * CHUNK
ATT_WIN = (N_BACK_CHUNKS + 2) * CHUNK
ATT_NSHIFT = (N_BACK_CHUNKS * CHUNK) // ATT_TQ + 1

S5_GROUPS_PER_BLOCK = MXU_DIM // SSM_GROUP
S5_BLOCK_STATES = S5_GROUPS_PER_BLOCK * SSM_STATE
S5_STRIP = 512
S5_TC = 32

MOE_TM = 256
MOE_NTS = 4
MOE_TF = 512
DISPATCH_TM = 512
COMBINE_TM = 256


def _cparams(sem):
    return pltpu.CompilerParams(dimension_semantics=sem, vmem_limit_bytes=VMEM_LIMIT_BYTES)


def _rms(x, g):
    r = lax.rsqrt(jnp.mean(x * x, axis=-1, keepdims=True) + EPS)
    return x * r * g


def _to_slab(dst_ref, src_ref, n_groups):
    pieces = src_ref.shape[-1] // LANES

    def group(g, c):
        r0 = pl.multiple_of(g * SUBLANES, SUBLANES)
        for p in range(pieces):
            dst_ref[pl.ds(r0 * pieces + p, SUBLANES, stride=pieces), :] = (
                src_ref[pl.ds(r0, SUBLANES), p * LANES:(p + 1) * LANES])
        return c

    lax.fori_loop(0, n_groups, group, 0)


def _from_slab(dst_ref, slab_ref, n_groups):
    pieces = dst_ref.shape[-1] // LANES
    rows = 2 * SUBLANES

    def group(g, c):
        r0 = pl.multiple_of(g * rows, rows)
        for p in range(pieces):
            lo = slab_ref[pl.ds(r0 * pieces + p, SUBLANES, stride=pieces), :]
            hi = slab_ref[pl.ds((r0 + SUBLANES) * pieces + p, SUBLANES, stride=pieces), :]
            dst_ref[pl.ds(r0, rows), p * LANES:(p + 1) * LANES] = (
                jnp.concatenate([lo, hi], axis=0).astype(dst_ref.dtype))
        return c

    lax.fori_loop(0, n_groups, group, 0)


def _inproj_kernel(x_ref, g_ref, w_ref, o_ref, nb_ref):
    @pl.when(pl.program_id(1) == 0)
    def _():
        nb_ref[...] = _rms(x_ref[...], g_ref[...]).astype(BF16)

    o_ref[...] = jnp.dot(nb_ref[...], w_ref[...], preferred_element_type=F32).astype(o_ref.dtype)


def _inproj(x2d, g, w_bf16, tm, tn):
    t, d = x2d.shape
    n = w_bf16.shape[1]
    return pl.pallas_call(
        _inproj_kernel,
        grid=(t // tm, n // tn),
        in_specs=[
            pl.BlockSpec((tm, d), lambda i, j: (i, 0)),
            pl.BlockSpec((1, d), lambda i, j: (0, 0)),
            pl.BlockSpec((d, tn), lambda i, j: (0, j)),
        ],
        out_specs=pl.BlockSpec((tm, tn), lambda i, j: (i, j)),
        out_shape=jax.ShapeDtypeStruct((t, n), BF16),
        scratch_shapes=[pltpu.VMEM((tm, d), BF16)],
        compiler_params=_cparams(("arbitrary", "arbitrary")),
        name="inproj",
    )(x2d, g.reshape(1, d), w_bf16)


def _attn_bias_table(rel_bias):
    n_heads = rel_bias.shape[0]
    period = ATT_WIN + ATT_TQ
    m = jnp.arange(period)
    diff = jnp.where(m < ATT_WIN, m, m - period)
    sh = jnp.arange(ATT_NSHIFT)[:, None]
    rel = N_BACK_CHUNKS * CHUNK - ATT_TQ * sh - diff[None, :]
    idx = jnp.clip(rel, -REL_CLIP, REL_CLIP) + REL_CLIP
    vext = rel_bias.astype(F32)[:, idx]
    flat = jnp.tile(vext, (1, 1, ATT_TQ))[..., :ATT_TQ * (period - 1)]
    tbl = flat.reshape(n_heads, ATT_NSHIFT, ATT_TQ, period - 1)[..., :ATT_WIN]
    return jnp.transpose(tbl, (1, 0, 2, 3))


def _attn_kernel(q_ref, k_ref, v_ref, b_ref, g_ref, o_ref, a_ref, *, n_heads):
    i = pl.program_id(1)
    s0 = pl.multiple_of(jnp.maximum(i * ATT_TQ - N_BACK_CHUNKS * CHUNK, 0), ATT_TQ)
    row = lax.broadcasted_iota(I32, (ATT_TQ, ATT_WIN), 0)
    col = lax.broadcasted_iota(I32, (ATT_TQ, ATT_WIN), 1)
    qc = (i * ATT_TQ + row) // CHUNK
    kc = (s0 + col) // CHUNK
    ok = (kc <= qc) & (kc >= qc - N_BACK_CHUNKS)
    scale = 1.0 / math.sqrt(HEAD_DIM)
    for h in range(n_heads):
        hs = slice(h * HEAD_DIM, (h + 1) * HEAD_DIM)
        qh = q_ref[:, hs]
        kh = k_ref[pl.ds(s0, ATT_WIN), hs]
        vh = v_ref[pl.ds(s0, ATT_WIN), hs]
        s = lax.dot_general(qh, kh, (((1,), (1,)), ((), ())), preferred_element_type=F32)
        s = jnp.where(ok, s * scale + b_ref[h], NEG_BIG)
        m = jnp.max(s, axis=-1, keepdims=True)
        p = jnp.exp(s - m)
        l = jnp.sum(p, axis=-1, keepdims=True)
        o = jnp.dot(p.astype(BF16), vh, preferred_element_type=F32)
        a_ref[:, hs] = o / l
    o_ref[...] = _rms(a_ref[...], g_ref[...]).astype(o_ref.dtype)


def _attention(proj3, bias_tbl, g, n_heads):
    b, l, _ = proj3.shape
    aw = n_heads * HEAD_DIM
    nsh = ATT_NSHIFT
    return pl.pallas_call(
        functools.partial(_attn_kernel, n_heads=n_heads),
        grid=(b, l // ATT_TQ),
        in_specs=[
            pl.BlockSpec((None, ATT_TQ, aw), lambda bi, i: (bi, i, 0)),
            pl.BlockSpec((None, l, aw), lambda bi, i: (bi, 0, 1)),
            pl.BlockSpec((None, l, aw), lambda bi, i: (bi, 0, 2)),
            pl.BlockSpec((None, n_heads, ATT_TQ, ATT_WIN),
                         lambda bi, i: (jnp.maximum(nsh - 1 - i, 0), 0, 0, 0)),
            pl.BlockSpec((1, aw), lambda bi, i: (0, 0)),
        ],
        out_specs=pl.BlockSpec((None, ATT_TQ, aw), lambda bi, i: (bi, i, 0)),
        out_shape=jax.ShapeDtypeStruct((b, l, aw), BF16),
        scratch_shapes=[pltpu.VMEM((ATT_TQ, aw), F32)],
        compiler_params=_cparams(("arbitrary", "arbitrary")),
        name="attn",
    )(proj3, proj3, proj3, bias_tbl, g.reshape(1, aw))


def _s5_params(lam_re, lam_im, log_step, b_re, b_im, c_re, c_im):
    g, p = lam_re.shape
    nb = g // S5_GROUPS_PER_BLOCK
    gl = S5_GROUPS_PER_BLOCK
    dt = jnp.exp(log_step.astype(F32))[:, None]
    lr = lam_re.astype(F32)
    li = lam_im.astype(F32)
    mag = jnp.exp(lr * dt)
    ab_re = mag * jnp.cos(li * dt)
    ab_im = mag * jnp.sin(li * dt)
    den = lr * lr + li * li
    nr = ab_re - 1.0
    ni = ab_im
    f_re = (nr * lr + ni * li) / den
    f_im = (ni * lr - nr * li) / den
    br = b_re.astype(F32)
    bi = b_im.astype(F32)
    bb_re = f_re[..., None] * br - f_im[..., None] * bi
    bb_im = f_re[..., None] * bi + f_im[..., None] * br
    eye = jnp.eye(gl, dtype=F32)
    bb = jnp.stack([bb_re, bb_im]).reshape(2, nb, gl, p, SSM_GROUP)
    wbd = jnp.einsum("ajgpc,gh->jgcahp", bb, eye).reshape(nb, gl * SSM_GROUP, 2 * gl * p)
    cc = jnp.stack([c_re.astype(F32), -c_im.astype(F32)]).reshape(2, nb, gl, SSM_GROUP, p)
    cbd = jnp.einsum("ajgcp,gh->jagphc", cc, eye).reshape(nb, 2 * gl * p, gl * SSM_GROUP)
    avec = jnp.stack([ab_re, ab_im]).reshape(2, nb, gl, p)
    avec = jnp.transpose(avec, (1, 0, 2, 3)).reshape(1, 2 * g * p)
    return wbd.astype(BF16), cbd.astype(BF16), avec


def _s5_kernel(u_ref, pm_ref, pmt_ref, wbd_ref, cbd_ref, av_ref, d_ref, wglu_ref, bglu_ref,
               g_ref, o_ref, xs_ref, hst_ref, *, n_blocks, tc, batch):
    rows = tc * batch
    sw = u_ref.shape[-1]
    bw = 2 * S5_BLOCK_STATES
    cw = S5_GROUPS_PER_BLOCK * SSM_GROUP

    @pl.when(pl.program_id(0) == 0)
    def _():
        hst_ref[...] = jnp.zeros_like(hst_ref)

    u_tm = jnp.dot(pm_ref[...], u_ref[...].reshape(rows, sw), preferred_element_type=F32)
    u_tmb = u_tm.astype(BF16)
    for j in range(n_blocks):
        xs_ref[:, j * bw:(j + 1) * bw] = jnp.dot(
            u_tmb[:, j * cw:(j + 1) * cw], wbd_ref[j], preferred_element_type=F32)

    for j in range(n_blocks):
        for s in range(S5_BLOCK_STATES // S5_STRIP):
            cr = j * bw + s * S5_STRIP
            ci = cr + S5_BLOCK_STATES
            a_re = jnp.broadcast_to(av_ref[:, cr:cr + S5_STRIP], (batch, S5_STRIP))
            a_im = jnp.broadcast_to(av_ref[:, ci:ci + S5_STRIP], (batch, S5_STRIP))

            def step(t, carry, cr=cr, ci=ci, a_re=a_re, a_im=a_im):
                h_re, h_im = carry
                r0 = pl.multiple_of(t * batch, batch)
                n_re = a_re * h_re - a_im * h_im + xs_ref[pl.ds(r0, batch), cr:cr + S5_STRIP]
                n_im = a_re * h_im + a_im * h_re + xs_ref[pl.ds(r0, batch), ci:ci + S5_STRIP]
                xs_ref[pl.ds(r0, batch), cr:cr + S5_STRIP] = n_re
                xs_ref[pl.ds(r0, batch), ci:ci + S5_STRIP] = n_im
                return n_re, n_im

            h_re, h_im = lax.fori_loop(
                0, tc, step, (hst_ref[:, cr:cr + S5_STRIP], hst_ref[:, ci:ci + S5_STRIP]), unroll=4)
            hst_ref[:, cr:cr + S5_STRIP] = h_re
            hst_ref[:, ci:ci + S5_STRIP] = h_im

    ys = [jnp.dot(xs_ref[:, j * bw:(j + 1) * bw].astype(BF16), cbd_ref[j], preferred_element_type=F32)
          for j in range(n_blocks)]
    y = jnp.concatenate(ys, axis=1) + d_ref[...] * u_tm
    y = jax.nn.gelu(y)
    z = jnp.dot(y.astype(BF16), wglu_ref[...], preferred_element_type=F32) + bglu_ref[...]
    y = y * jax.nn.sigmoid(z)
    res = _rms(y, g_ref[...]).astype(BF16)
    out_bm = jnp.dot(pmt_ref[...], res, preferred_element_type=F32).astype(o_ref.dtype)
    o_ref[...] = out_bm.reshape(batch, tc, sw)


def _s5(proj3, wbd, cbd, avec, d, wglu_bf16, bglu, g, u_block_index):
    b, l, _ = proj3.shape
    assert b == SUBLANES, "the scan keeps the batch on the sublane axis"
    n_blocks, cw, bw = wbd.shape
    sw = n_blocks * cw
    tc = S5_TC
    rows = tc * b
    r = jnp.arange(rows)
    src = (r % b) * tc + r // b
    pm = (src[:, None] == jnp.arange(rows)[None, :]).astype(BF16)
    pmt = pm.T
    const = lambda shape: pl.BlockSpec(shape, lambda c: (0,) * len(shape))
    return pl.pallas_call(
        functools.partial(_s5_kernel, n_blocks=n_blocks, tc=tc, batch=b),
        grid=(l // tc,),
        in_specs=[
            pl.BlockSpec((b, tc, sw), lambda c: (0, c, u_block_index)),
            const((rows, rows)), const((rows, rows)),
            const((n_blocks, cw, bw)), const((n_blocks, bw, cw)),
            const((1, n_blocks * bw)), const((1, sw)),
            const((sw, sw)), const((1, sw)), const((1, sw)),
        ],
        out_specs=pl.BlockSpec((b, tc, sw), lambda c: (0, c, 0)),
        out_shape=jax.ShapeDtypeStruct((b, l, sw), BF16),
        scratch_shapes=[pltpu.VMEM((rows, n_blocks * bw), F32),
                        pltpu.VMEM((b, n_blocks * bw), F32)],
        compiler_params=_cparams(("arbitrary",)),
        name="s5",
    )(proj3, pm, pmt, wbd, cbd, avec, d.reshape(1, sw), wglu_bf16, bglu.reshape(1, sw),
      g.reshape(1, sw))


def _outproj_kernel(ma_ref, ms_ref, x_ref, wo_ref, g2_ref, wr_ref, br_ref, tri_ref,
                    h_ref, n2s_ref, ri_ref, rf_ref, cnt_ref, carry_ref, n2_ref):
    aw = ma_ref.shape[-1]

    @pl.when(pl.program_id(0) == 0)
    def _():
        carry_ref[...] = jnp.zeros_like(carry_ref)

    h = (x_ref[...]
         + jnp.dot(ma_ref[...], wo_ref[:aw, :], preferred_element_type=F32)
         + jnp.dot(ms_ref[...], wo_ref[aw:, :], preferred_element_type=F32))
    h_ref[...] = h
    n2 = _rms(h, g2_ref[...])
    n2_ref[...] = n2
    _to_slab(n2s_ref, n2_ref, n2.shape[0] // SUBLANES)

    n_hi = n2.astype(BF16)
    n_lo = (n2 - n_hi.astype(F32)).astype(BF16)
    w = wr_ref[...]
    w_hi = w.astype(BF16)
    w_lo = (w - w_hi.astype(F32)).astype(BF16)
    logits = (jnp.dot(n_hi, w_hi, preferred_element_type=F32)
              + jnp.dot(n_lo, w_hi, preferred_element_type=F32)
              + jnp.dot(n_hi, w_lo, preferred_element_type=F32)) + br_ref[...]

    tm, ne = logits.shape
    lane = lax.broadcasted_iota(I32, (tm, ne), 1)
    work = logits
    vals, idxs = [], []
    for _ in range(TOP_K):
        m = jnp.max(work, axis=-1, keepdims=True)
        ik = jnp.min(jnp.where(work == m, lane, ne), axis=-1, keepdims=True)
        vals.append(m)
        idxs.append(ik)
        work = jnp.where(lane == ik, -jnp.inf, work)
    es = [jnp.exp(v - vals[0]) for v in vals]
    den = es[0] + es[1] + es[2] + es[3]

    hot = jnp.zeros((tm, ne), F32)
    for ik in idxs:
        hot = hot + (lane == ik).astype(F32)
    before = jnp.dot(tri_ref[...], hot.astype(BF16), preferred_element_type=F32) + carry_ref[...]
    ri = jnp.zeros((tm, ne), I32)
    rf = jnp.zeros((tm, ne), F32)
    for k in range(TOP_K):
        rank = jnp.sum(jnp.where(lane == idxs[k], before, 0.0), axis=-1, keepdims=True)
        ri = jnp.where(lane == k, idxs[k], ri)
        ri = jnp.where(lane == TOP_K + k, rank.astype(I32), ri)
        rf = jnp.where(lane == k, es[k] / den, rf)
    ri_ref[...] = ri
    rf_ref[...] = rf
    carry_ref[...] = carry_ref[...] + jnp.sum(hot, axis=0, keepdims=True)
    cnt_ref[...] = carry_ref[...]


def _outproj(ma, ms, x2d, wo_bf16, g2, w_router, b_router, tm):
    t, d = x2d.shape
    aw = ma.shape[1]
    ne = w_router.shape[1]
    pieces = d // LANES
    wr = jnp.zeros((d, LANES), F32).at[:, :ne].set(w_router.astype(F32))
    br = jnp.full((1, LANES), NEG_BIG, F32).at[0, :ne].set(b_router.astype(F32))
    tri = (jnp.arange(tm)[:, None] > jnp.arange(tm)[None, :]).astype(BF16)
    const = lambda shape: pl.BlockSpec(shape, lambda i: (0,) * len(shape))
    row = lambda w: pl.BlockSpec((tm, w), lambda i: (i, 0))
    return pl.pallas_call(
        _outproj_kernel,
        grid=(t // tm,),
        in_specs=[row(aw), row(ms.shape[1]), row(d), const((d, d)), const((1, d)),
                  const((d, LANES)), const((1, LANES)), const((tm, tm))],
        out_specs=[row(d), pl.BlockSpec((tm * pieces, LANES), lambda i: (i, 0)),
                   row(LANES), row(LANES), const((1, LANES))],
        out_shape=[jax.ShapeDtypeStruct((t, d), F32), jax.ShapeDtypeStruct((t * pieces, LANES), F32),
                   jax.ShapeDtypeStruct((t, LANES), I32), jax.ShapeDtypeStruct((t, LANES), F32),
                   jax.ShapeDtypeStruct((1, LANES), F32)],
        scratch_shapes=[pltpu.VMEM((1, LANES), F32), pltpu.VMEM((tm, d), F32)],
        compiler_params=_cparams(("arbitrary",)),
        name="outproj",
    )(ma, ms, x2d, wo_bf16, g2.reshape(1, d), wr, br, tri)


def _route_tables(route_i, counts, n_tokens, n_experts):
    idx = route_i[:, :TOP_K]
    rank = route_i[:, TOP_K:2 * TOP_K]
    n_tiles_max = n_tokens * TOP_K // MOE_TM + n_experts
    s_max = n_experts + n_tiles_max // MOE_NTS
    e_iota = jnp.arange(n_experts, dtype=I32)

    tiles_e = (counts + MOE_TM - 1) // MOE_TM
    tile_end = jnp.cumsum(tiles_e)
    tile_off = tile_end - tiles_e
    off_of = jnp.sum(jnp.where(idx[..., None] == e_iota, tile_off, 0), axis=-1)
    pos = (off_of * MOE_TM + rank).astype(I32)

    nst_e = (tiles_e + MOE_NTS - 1) // MOE_NTS
    st_incl = jnp.cumsum(nst_e)
    num_st = st_incl[-1]
    s = jnp.arange(s_max, dtype=I32)
    active = s < num_st
    s_eff = jnp.minimum(s, num_st - 1)
    e_s = jnp.minimum(jnp.sum(st_incl[None, :] <= s_eff[:, None], axis=1), n_experts - 1).astype(I32)
    onehot = e_s[:, None] == e_iota[None, :]
    pick = lambda v: jnp.sum(jnp.where(onehot, v[None, :], 0), axis=1)
    local = s_eff - (pick(st_incl) - pick(nst_e))
    tile0 = (pick(tile_off) + MOE_NTS * local).astype(I32)
    nt = jnp.where(active, jnp.minimum(MOE_NTS, pick(tiles_e) - MOE_NTS * local), 0).astype(I32)

    tile_ids = jnp.arange(n_tiles_max, dtype=I32)
    is_last = jnp.any((tile_ids[:, None] == tile_end[None, :] - 1) & (tiles_e[None, :] > 0), axis=1)
    pad_tile = (is_last | (tile_ids >= tile_end[-1])).astype(I32)
    return pos, e_s, tile0, nt, pad_tile, tile_end[-1:].astype(I32)


def _dispatch_kernel(pad_tile, pos_ref, n2_ref, xs_hbm, zero_ref, sem_z, sem_r, *,
                     tokens_per_step, pieces):
    i = pl.program_id(0)
    tile_rows = MOE_TM * pieces
    n_tiles = pad_tile.shape[0]

    @pl.when(i == 0)
    def _zero_fill():
        zero_ref[...] = jnp.zeros_like(zero_ref)

        def fill(tl):
            start = pl.multiple_of(tl * tile_rows, tile_rows)
            return pltpu.make_async_copy(zero_ref, xs_hbm.at[pl.ds(start, tile_rows)], sem_z)

        def issue(tl, c):
            @pl.when(pad_tile[tl] > 0)
            def _():
                fill(tl).start()
            return c

        def drain(tl, c):
            @pl.when(pad_tile[tl] > 0)
            def _():
                fill(tl).wait()
            return c

        lax.fori_loop(0, n_tiles, issue, 0)
        lax.fori_loop(0, n_tiles, drain, 0)

    def row_copy(tok, dst):
        return pltpu.make_async_copy(
            n2_ref.at[pl.ds(pl.multiple_of(tok * pieces, pieces), pieces)],
            xs_hbm.at[pl.ds(pl.multiple_of(dst * pieces, pieces), pieces)], sem_r)

    def table_row(m, c):
        def one(b, c2):
            a = m * LANES + b
            row_copy(lax.shift_right_logical(a, 2), pos_ref[m, b]).start()
            return c2

        lax.fori_loop(0, LANES, one, 0, unroll=8)
        return c

    n_table_rows = tokens_per_step * TOP_K // LANES
    lax.fori_loop(0, n_table_rows, table_row, 0)

    def drain_table_row(m, c):
        for _ in range(LANES):
            row_copy(0, 0).wait()
        return c

    lax.fori_loop(0, n_table_rows, drain_table_row, 0)


def _dispatch(n2_slab, pos, pad_tile, d):
    pieces = d // LANES
    n_tokens = n2_slab.shape[0] // pieces
    n_tiles = pad_tile.shape[0]
    tm = min(DISPATCH_TM, n_tokens)
    table_rows = tm * TOP_K // LANES
    grid_spec = pltpu.PrefetchScalarGridSpec(
        num_scalar_prefetch=1,
        grid=(n_tokens // tm,),
        in_specs=[pl.BlockSpec((table_rows, LANES), lambda i, pad: (i, 0), memory_space=pltpu.SMEM),
                  pl.BlockSpec((tm * pieces, LANES), lambda i, pad: (i, 0))],
        out_specs=pl.BlockSpec(memory_space=pl.ANY),
        scratch_shapes=[pltpu.VMEM((MOE_TM * pieces, LANES), F32),
                        pltpu.SemaphoreType.DMA(()), pltpu.SemaphoreType.DMA(())],
    )
    return pl.pallas_call(
        functools.partial(_dispatch_kernel, tokens_per_step=tm, pieces=pieces),
        grid_spec=grid_spec,
        out_shape=jax.ShapeDtypeStruct((n_tiles * MOE_TM * pieces, LANES), F32),
        compiler_params=_cparams(("arbitrary",)),
        name="dispatch",
    )(pad_tile, pos.reshape(-1, LANES), n2_slab)


def _experts_kernel(st_e, st_t0, st_nt, n_used, xs_hbm, wg_ref, wu_ref, wd_ref, bg_ref, bu_ref, bd_ref,
                    y_hbm, slab, xs_b, acc, wgb, wub, wdb, sem_g, sem_s, *, n_j, n_tiles):
    s = pl.program_id(0)
    j = pl.program_id(1)
    nt = st_nt[s]
    t0 = st_t0[s]
    pieces = acc.shape[-1] // LANES
    tile_rows = MOE_TM * pieces
    groups_per_tile = MOE_TM // SUBLANES

    def tile_in(q):
        return pltpu.make_async_copy(
            xs_hbm.at[pl.ds(pl.multiple_of((t0 + q) * tile_rows, tile_rows), tile_rows)],
            slab.at[pl.ds(pl.multiple_of(q * tile_rows, tile_rows), tile_rows)], sem_g)

    def tile_out(q, tl):
        return pltpu.make_async_copy(
            slab.at[pl.ds(pl.multiple_of(q * tile_rows, tile_rows), tile_rows)],
            y_hbm.at[pl.ds(pl.multiple_of(tl * tile_rows, tile_rows), tile_rows)], sem_s)

    def for_tiles(lo, hi, fn):
        def body(q, c):
            fn(q)
            return c
        lax.fori_loop(lo, hi, body, 0)

    @pl.when((s == 0) & (j == 0))
    def _zero_tail():
        slab[pl.ds(0, tile_rows), :] = jnp.zeros((tile_rows, LANES), F32)
        for_tiles(n_used[0], n_tiles, lambda tl: tile_out(0, tl).start())
        for_tiles(n_used[0], n_tiles, lambda tl: tile_out(0, tl).wait())

    @pl.when(nt > 0)
    def _active():
        @pl.when(j == 0)
        def _load():
            for_tiles(0, nt, lambda q: tile_in(q).start())
            for_tiles(0, nt, lambda q: tile_in(q).wait())
            _from_slab(xs_b, slab, nt * (groups_per_tile // 2))

        wgb[...] = wg_ref[...].astype(BF16)
        wub[...] = wu_ref[...].astype(BF16)
        wdb[...] = wd_ref[...].astype(BF16)

        def tile(q, c):
            r0 = pl.multiple_of(q * MOE_TM, MOE_TM)
            xt = xs_b[pl.ds(r0, MOE_TM), :]
            g = jnp.dot(xt, wgb[...], preferred_element_type=F32) + bg_ref[...]
            u = jnp.dot(xt, wub[...], preferred_element_type=F32) + bu_ref[...]
            g = jnp.minimum(g, SWIGLU_LIMIT)
            u = jnp.clip(u, -SWIGLU_LIMIT, SWIGLU_LIMIT)
            hdn = g * jax.nn.sigmoid(SWIGLU_ALPHA * g) * (u + 1.0)
            part = jnp.dot(hdn.astype(BF16), wdb[...], preferred_element_type=F32)

            @pl.when(j == 0)
            def _():
                acc[pl.ds(r0, MOE_TM), :] = part + bd_ref[...]

            @pl.when(j > 0)
            def _():
                acc[pl.ds(r0, MOE_TM), :] = acc[pl.ds(r0, MOE_TM), :] + part

            return c

        lax.fori_loop(0, nt, tile, 0)

        @pl.when(j == n_j - 1)
        def _store():
            _to_slab(slab, acc, nt * groups_per_tile)
            for_tiles(0, nt, lambda q: tile_out(q, t0 + q).start())
            for_tiles(0, nt, lambda q: tile_out(q, t0 + q).wait())


def _experts(xs_slab, tables, w_gate, b_gate, w_up, b_up, w_down, b_down):
    st_e, st_t0, st_nt, n_used = tables
    ne, d, f = w_gate.shape
    pieces = d // LANES
    n_tiles = xs_slab.shape[0] // (MOE_TM * pieces)
    n_j = f // MOE_TF
    s_max = st_e.shape[0]
    super_rows = MOE_NTS * MOE_TM

    def w_in_map(s, j, st_e, st_t0, st_nt, n_used):
        return (st_e[s], 0, jnp.where(st_nt[s] > 0, j, n_j - 1))

    def w_down_map(s, j, st_e, st_t0, st_nt, n_used):
        return (st_e[s], jnp.where(st_nt[s] > 0, j, n_j - 1), 0)

    def b_down_map(s, j, st_e, st_t0, st_nt, n_used):
        return (st_e[s], 0, 0)

    any_spec = pl.BlockSpec(memory_space=pl.ANY)
    grid_spec = pltpu.PrefetchScalarGridSpec(
        num_scalar_prefetch=4,
        grid=(s_max, n_j),
        in_specs=[
            any_spec,
            pl.BlockSpec((None, d, MOE_TF), w_in_map),
            pl.BlockSpec((None, d, MOE_TF), w_in_map),
            pl.BlockSpec((None, MOE_TF, d), w_down_map),
            pl.BlockSpec((None, 1, MOE_TF), w_in_map),
            pl.BlockSpec((None, 1, MOE_TF), w_in_map),
            pl.BlockSpec((None, 1, d), b_down_map),
        ],
        out_specs=any_spec,
        scratch_shapes=[
            pltpu.VMEM((super_rows * pieces, LANES), F32),
            pltpu.VMEM((super_rows, d), BF16),
            pltpu.VMEM((super_rows, d), F32),
            pltpu.VMEM((d, MOE_TF), BF16),
            pltpu.VMEM((d, MOE_TF), BF16),
            pltpu.VMEM((MOE_TF, d), BF16),
            pltpu.SemaphoreType.DMA(()),
            pltpu.SemaphoreType.DMA(()),
        ],
    )
    return pl.pallas_call(
        functools.partial(_experts_kernel, n_j=n_j, n_tiles=n_tiles),
        grid_spec=grid_spec,
        out_shape=jax.ShapeDtypeStruct(xs_slab.shape, F32),
        compiler_params=_cparams(("arbitrary", "arbitrary")),
        name="experts",
    )(st_e, st_t0, st_nt, n_used, xs_slab,
      w_gate, w_up, w_down, b_gate.reshape(ne, 1, f), b_up.reshape(ne, 1, f),
      b_down.reshape(ne, 1, d))


def _combine_kernel(pos_ref, h_ref, rf_ref, g_ref, y_hbm, o_ref, yv, acc_ref, sem, *, final_norm):
    tm, d = h_ref.shape
    pieces = d // LANES

    def row_copy(src, dst):
        return pltpu.make_async_copy(
            y_hbm.at[pl.ds(pl.multiple_of(src * pieces, pieces), pieces)],
            yv.at[pl.ds(pl.multiple_of(dst * pieces, pieces), pieces)], sem)

    def table_row(m, c):
        def one(b, c2):
            a = m * LANES + b
            row_copy(pos_ref[m, b], (a & (TOP_K - 1)) * tm + lax.shift_right_logical(a, 2)).start()
            return c2

        lax.fori_loop(0, LANES, one, 0, unroll=8)
        return c

    lax.fori_loop(0, tm * TOP_K // LANES, table_row, 0)

    def drain_table_row(m, c):
        for _ in range(LANES):
            row_copy(0, 0).wait()
        return c

    lax.fori_loop(0, tm * TOP_K // LANES, drain_table_row, 0)

    def group(g, c):
        r0 = pl.multiple_of(g * SUBLANES, SUBLANES)
        rf = rf_ref[pl.ds(r0, SUBLANES), :]
        gates = [jnp.broadcast_to(rf[:, k:k + 1], (SUBLANES, LANES)) for k in range(TOP_K)]
        for p in range(pieces):
            v = h_ref[pl.ds(r0, SUBLANES), p * LANES:(p + 1) * LANES]
            for k in range(TOP_K):
                v = v + gates[k] * yv[pl.ds((k * tm + r0) * pieces + p, SUBLANES, stride=pieces), :]
            acc_ref[pl.ds(r0, SUBLANES), p * LANES:(p + 1) * LANES] = v
        return c

    lax.fori_loop(0, tm // SUBLANES, group, 0)
    o_ref[...] = _rms(acc_ref[...], g_ref[...]) if final_norm else acc_ref[...]


def _combine(h, y_slab, pos, route_f, g, final_norm):
    t, d = h.shape
    pieces = d // LANES
    tm = min(COMBINE_TM, t)
    table_rows = tm * TOP_K // LANES
    return pl.pallas_call(
        functools.partial(_combine_kernel, final_norm=final_norm),
        grid=(t // tm,),
        in_specs=[pl.BlockSpec((table_rows, LANES), lambda i: (i, 0), memory_space=pltpu.SMEM),
                  pl.BlockSpec((tm, d), lambda i: (i, 0)),
                  pl.BlockSpec((tm, LANES), lambda i: (i, 0)),
                  pl.BlockSpec((1, d), lambda i: (0, 0)),
                  pl.BlockSpec(memory_space=pl.ANY)],
        out_specs=pl.BlockSpec((tm, d), lambda i: (i, 0)),
        out_shape=jax.ShapeDtypeStruct((t, d), F32),
        scratch_shapes=[pltpu.VMEM((TOP_K * tm * pieces, LANES), F32), pltpu.VMEM((tm, d), F32),
                        pltpu.SemaphoreType.DMA(())],
        compiler_params=_cparams(("arbitrary",)),
        name="combine",
    )(pos.reshape(-1, LANES), h, route_f, g.reshape(1, d), y_slab)


def kernel(x, norm1_g, w_in, rel_bias, ssm_lambda_re, ssm_lambda_im, ssm_log_step, ssm_b_re, ssm_b_im, ssm_c_re, ssm_c_im, ssm_d, w_ssm_glu, b_ssm_glu, attn_out_g, ssm_out_g, w_out, norm2_g, w_router, b_router, w_gate, b_gate, w_up, b_up, w_down, b_down, norm_f_g):
    b, l, d = x.shape
    depth = w_in.shape[0]
    aw = attn_out_g.shape[-1]
    n_heads = aw // HEAD_DIM
    n_experts = w_router.shape[-1]
    t = b * l
    assert l % ATT_TQ == 0 and l >= ATT_WIN and l % S5_TC == 0
    assert (t * TOP_K) % (MOE_NTS * MOE_TM) == 0 and t % DISPATCH_TM == 0

    h = x.reshape(t, d).astype(F32)
    for li in range(depth):
        last = li == depth - 1
        proj = _inproj(h, norm1_g[li], w_in[li].astype(BF16), tm=min(1024, t), tn=512)
        proj3 = proj.reshape(b, l, -1)
        mixed_a = _attention(proj3, _attn_bias_table(rel_bias[li]), attn_out_g[li], n_heads)
        wbd, cbd, avec = _s5_params(ssm_lambda_re[li], ssm_lambda_im[li], ssm_log_step[li],
                                    ssm_b_re[li], ssm_b_im[li], ssm_c_re[li], ssm_c_im[li])
        mixed_s = _s5(proj3, wbd, cbd, avec, ssm_d[li], w_ssm_glu[li].astype(BF16), b_ssm_glu[li],
                      ssm_out_g[li], u_block_index=3 * aw // (d - aw))
        h, n2_slab, route_i, route_f, cnt = _outproj(
            mixed_a.reshape(t, aw), mixed_s.reshape(t, d - aw), h, w_out[li].astype(BF16),
            norm2_g[li], w_router[li], b_router[li], tm=min(512, t))
        pos, st_e, st_t0, st_nt, pad_tile, n_used = _route_tables(
            route_i, cnt[0, :n_experts].astype(I32), t, n_experts)
        xs_slab = _dispatch(n2_slab, pos, pad_tile, d)
        ys_slab = _experts(xs_slab, (st_e, st_t0, st_nt, n_used), w_gate[li], b_gate[li], w_up[li],
                           b_up[li], w_down[li], b_down[li])
        g_fin = norm_f_g if last else jnp.ones((d,), F32)
        h = _combine(h, ys_slab, pos, route_f, g_fin, final_norm=last)
    return h.reshape(b, l, d).astype(x.dtype)
```

```python
import functools
import math

import jax
import jax.numpy as jnp
from jax import lax
from jax.experimental import pallas as pl
from jax.experimental.pallas import tpu as pltpu

F32 = jnp.float32
BF16 = jnp.bfloat16
I32 = jnp.int32

CHUNK = 64
HEAD_DIM = 128
N_BACK_CHUNKS = 8
REL_CLIP = 128
SSM_GROUP = 16
SSM_STATE = 64
TOP_K = 4
SWIGLU_LIMIT = 7.0
SWIGLU_ALPHA = 1.702
EPS = 1e-5
NEG_BIG = -1e30

LANES = 128
SUBLANES = 8
MXU_DIM = 256
VMEM_LIMIT_BYTES = 56 * 1024 * 1024

ATT_TQ = 2 * CHUNK
ATT_WIN = (N_BACK_CHUNKS + 2) * CHUNK
ATT_NSHIFT = (N_BACK_CHUNKS * CHUNK) // ATT_TQ + 1

S5_GROUPS_PER_BLOCK = MXU_DIM // SSM_GROUP
S5_BLOCK_STATES = S5_GROUPS_PER_BLOCK * SSM_STATE
S5_STRIP = 512
S5_TC = 32

MOE_TM = 256
MOE_NTS = 8
MOE_TF = 256
MOE_CHUNK_TILES = 4
DISPATCH_TM = 512
COMBINE_TM = 256


def _cparams(sem):
    return pltpu.CompilerParams(dimension_semantics=sem, vmem_limit_bytes=VMEM_LIMIT_BYTES)


def _rms(x, g):
    r = lax.rsqrt(jnp.mean(x * x, axis=-1, keepdims=True) + EPS)
    return x * r * g


def _to_slab(slab_ref, slab_row0, src_ref, src_row0, n_groups):
    pieces = src_ref.shape[-1] // LANES

    def group(g, c):
        r0 = g * SUBLANES
        src0 = pl.multiple_of(src_row0 + r0, SUBLANES)
        for p in range(pieces):
            slab_ref[pl.ds((slab_row0 + r0) * pieces + p, SUBLANES, stride=pieces), :] = (
                src_ref[pl.ds(src0, SUBLANES), p * LANES:(p + 1) * LANES])
        return c

    lax.fori_loop(0, n_groups, group, 0)


def _from_slab(dst_ref, dst_row0, slab_ref, slab_row0, n_groups):
    pieces = dst_ref.shape[-1] // LANES
    rows = 2 * SUBLANES

    def group(g, c):
        r0 = g * rows
        dst0 = pl.multiple_of(dst_row0 + r0, rows)
        for p in range(pieces):
            lo = slab_ref[pl.ds((slab_row0 + r0) * pieces + p, SUBLANES, stride=pieces), :]
            hi = slab_ref[pl.ds((slab_row0 + r0 + SUBLANES) * pieces + p, SUBLANES, stride=pieces), :]
            dst_ref[pl.ds(dst0, rows), p * LANES:(p + 1) * LANES] = (
                jnp.concatenate([lo, hi], axis=0).astype(dst_ref.dtype))
        return c

    lax.fori_loop(0, n_groups, group, 0)


def _inproj_kernel(x_ref, g_ref, w_ref, o_ref, nb_ref):
    @pl.when(pl.program_id(1) == 0)
    def _():
        nb_ref[...] = _rms(x_ref[...], g_ref[...]).astype(BF16)

    o_ref[...] = jnp.dot(nb_ref[...], w_ref[...], preferred_element_type=F32).astype(o_ref.dtype)


def _inproj(x2d, g, w_bf16, tm, tn):
    t, d = x2d.shape
    n = w_bf16.shape[1]
    return pl.pallas_call(
        _inproj_kernel,
        grid=(t // tm, n // tn),
        in_specs=[
            pl.BlockSpec((tm, d), lambda i, j: (i, 0)),
            pl.BlockSpec((1, d), lambda i, j: (0, 0)),
            pl.BlockSpec((d, tn), lambda i, j: (0, j)),
        ],
        out_specs=pl.BlockSpec((tm, tn), lambda i, j: (i, j)),
        out_shape=jax.ShapeDtypeStruct((t, n), BF16),
        scratch_shapes=[pltpu.VMEM((tm, d), BF16)],
        compiler_params=_cparams(("arbitrary", "arbitrary")),
        name="inproj",
    )(x2d, g.reshape(1, d), w_bf16)


def _attn_bias_table(rel_bias):
    n_heads = rel_bias.shape[0]
    period = ATT_WIN + ATT_TQ
    m = jnp.arange(period)
    diff = jnp.where(m < ATT_WIN, m, m - period)
    sh = jnp.arange(ATT_NSHIFT)[:, None]
    rel = N_BACK_CHUNKS * CHUNK - ATT_TQ * sh - diff[None, :]
    idx = jnp.clip(rel, -REL_CLIP, REL_CLIP) + REL_CLIP
    vext = rel_bias.astype(F32)[:, idx]
    flat = jnp.tile(vext, (1, 1, ATT_TQ))[..., :ATT_TQ * (period - 1)]
    tbl = flat.reshape(n_heads, ATT_NSHIFT, ATT_TQ, period - 1)[..., :ATT_WIN]
    return jnp.transpose(tbl, (1, 0, 2, 3))


def _attn_kernel(q_ref, k_ref, v_ref, b_ref, g_ref, o_ref, a_ref, *, n_heads):
    i = pl.program_id(1)
    s0 = pl.multiple_of(jnp.maximum(i * ATT_TQ - N_BACK_CHUNKS * CHUNK, 0), ATT_TQ)
    row = lax.broadcasted_iota(I32, (ATT_TQ, ATT_WIN), 0)
    col = lax.broadcasted_iota(I32, (ATT_TQ, ATT_WIN), 1)
    qc = (i * ATT_TQ + row) // CHUNK
    kc = (s0 + col) // CHUNK
    ok = (kc <= qc) & (kc >= qc - N_BACK_CHUNKS)
    scale = 1.0 / math.sqrt(HEAD_DIM)
    for h in range(n_heads):
        hs = slice(h * HEAD_DIM, (h + 1) * HEAD_DIM)
        qh = q_ref[:, hs]
        kh = k_ref[pl.ds(s0, ATT_WIN), hs]
        vh = v_ref[pl.ds(s0, ATT_WIN), hs]
        s = lax.dot_general(qh, kh, (((1,), (1,)), ((), ())), preferred_element_type=F32)
        s = jnp.where(ok, s * scale + b_ref[h], NEG_BIG)
        m = jnp.max(s, axis=-1, keepdims=True)
        p = jnp.exp(s - m)
        l = jnp.sum(p, axis=-1, keepdims=True)
        o = jnp.dot(p.astype(BF16), vh, preferred_element_type=F32)
        a_ref[:, hs] = o / l
    o_ref[...] = _rms(a_ref[...], g_ref[...]).astype(o_ref.dtype)


def _attention(proj3, bias_tbl, g, n_heads):
    b, l, _ = proj3.shape
    aw = n_heads * HEAD_DIM
    nsh = ATT_NSHIFT
    return pl.pallas_call(
        functools.partial(_attn_kernel, n_heads=n_heads),
        grid=(b, l // ATT_TQ),
        in_specs=[
            pl.BlockSpec((None, ATT_TQ, aw), lambda bi, i: (bi, i, 0)),
            pl.BlockSpec((None, l, aw), lambda bi, i: (bi, 0, 1)),
            pl.BlockSpec((None, l, aw), lambda bi, i: (bi, 0, 2)),
            pl.BlockSpec((None, n_heads, ATT_TQ, ATT_WIN),
                         lambda bi, i: (jnp.maximum(nsh - 1 - i, 0), 0, 0, 0)),
            pl.BlockSpec((1, aw), lambda bi, i: (0, 0)),
        ],
        out_specs=pl.BlockSpec((None, ATT_TQ, aw), lambda bi, i: (bi, i, 0)),
        out_shape=jax.ShapeDtypeStruct((b, l, aw), BF16),
        scratch_shapes=[pltpu.VMEM((ATT_TQ, aw), F32)],
        compiler_params=_cparams(("arbitrary", "arbitrary")),
        name="attn",
    )(proj3, proj3, proj3, bias_tbl, g.reshape(1, aw))


def _s5_params(lam_re, lam_im, log_step, b_re, b_im, c_re, c_im):
    g, p = lam_re.shape
    nb = g // S5_GROUPS_PER_BLOCK
    gl = S5_GROUPS_PER_BLOCK
    dt = jnp.exp(log_step.astype(F32))[:, None]
    lr = lam_re.astype(F32)
    li = lam_im.astype(F32)
    mag = jnp.exp(lr * dt)
    ab_re = mag * jnp.cos(li * dt)
    ab_im = mag * jnp.sin(li * dt)
    den = lr * lr + li * li
    nr = ab_re - 1.0
    ni = ab_im
    f_re = (nr * lr + ni * li) / den
    f_im = (ni * lr - nr * li) / den
    br = b_re.astype(F32)
    bi = b_im.astype(F32)
    bb_re = f_re[..., None] * br - f_im[..., None] * bi
    bb_im = f_re[..., None] * bi + f_im[..., None] * br
    eye = jnp.eye(gl, dtype=F32)
    bb = jnp.stack([bb_re, bb_im]).reshape(2, nb, gl, p, SSM_GROUP)
    wbd = jnp.einsum("ajgpc,gh->jgcahp", bb, eye).reshape(nb, gl * SSM_GROUP, 2 * gl * p)
    cc = jnp.stack([c_re.astype(F32), -c_im.astype(F32)]).reshape(2, nb, gl, SSM_GROUP, p)
    cbd = jnp.einsum("ajgcp,gh->jagphc", cc, eye).reshape(nb, 2 * gl * p, gl * SSM_GROUP)
    avec = jnp.stack([ab_re, ab_im]).reshape(2, nb, gl, p)
    avec = jnp.transpose(avec, (1, 0, 2, 3)).reshape(1, 2 * g * p)
    return wbd.astype(BF16), cbd.astype(BF16), avec


def _s5_kernel(u_ref, pm_ref, pmt_ref, wbd_ref, cbd_ref, av_ref, d_ref, wglu_ref, bglu_ref,
               g_ref, o_ref, xs_ref, hst_ref, *, n_blocks, tc, batch):
    rows = tc * batch
    sw = u_ref.shape[-1]
    bw = 2 * S5_BLOCK_STATES
    cw = S5_GROUPS_PER_BLOCK * SSM_GROUP

    @pl.when(pl.program_id(0) == 0)
    def _():
        hst_ref[...] = jnp.zeros_like(hst_ref)

    u_tm = jnp.dot(pm_ref[...], u_ref[...].reshape(rows, sw), preferred_element_type=F32)
    u_tmb = u_tm.astype(BF16)
    for j in range(n_blocks):
        xs_ref[:, j * bw:(j + 1) * bw] = jnp.dot(
            u_tmb[:, j * cw:(j + 1) * cw], wbd_ref[j], preferred_element_type=F32)

    for j in range(n_blocks):
        for s in range(S5_BLOCK_STATES // S5_STRIP):
            cr = j * bw + s * S5_STRIP
            ci = cr + S5_BLOCK_STATES
            a_re = jnp.broadcast_to(av_ref[:, cr:cr + S5_STRIP], (batch, S5_STRIP))
            a_im = jnp.broadcast_to(av_ref[:, ci:ci + S5_STRIP], (batch, S5_STRIP))

            def step(t, carry, cr=cr, ci=ci, a_re=a_re, a_im=a_im):
                h_re, h_im = carry
                r0 = pl.multiple_of(t * batch, batch)
                n_re = a_re * h_re - a_im * h_im + xs_ref[pl.ds(r0, batch), cr:cr + S5_STRIP]
                n_im = a_re * h_im + a_im * h_re + xs_ref[pl.ds(r0, batch), ci:ci + S5_STRIP]
                xs_ref[pl.ds(r0, batch), cr:cr + S5_STRIP] = n_re
                xs_ref[pl.ds(r0, batch), ci:ci + S5_STRIP] = n_im
                return n_re, n_im

            h_re, h_im = lax.fori_loop(
                0, tc, step, (hst_ref[:, cr:cr + S5_STRIP], hst_ref[:, ci:ci + S5_STRIP]), unroll=4)
            hst_ref[:, cr:cr + S5_STRIP] = h_re
            hst_ref[:, ci:ci + S5_STRIP] = h_im

    ys = [jnp.dot(xs_ref[:, j * bw:(j + 1) * bw].astype(BF16), cbd_ref[j], preferred_element_type=F32)
          for j in range(n_blocks)]
    y = jnp.concatenate(ys, axis=1) + d_ref[...] * u_tm
    y = jax.nn.gelu(y)
    z = jnp.dot(y.astype(BF16), wglu_ref[...], preferred_element_type=F32) + bglu_ref[...]
    y = y * jax.nn.sigmoid(z)
    res = _rms(y, g_ref[...]).astype(BF16)
    out_bm = jnp.dot(pmt_ref[...], res, preferred_element_type=F32).astype(o_ref.dtype)
    o_ref[...] = out_bm.reshape(batch, tc, sw)


def _s5(proj3, wbd, cbd, avec, d, wglu_bf16, bglu, g, u_block_index):
    b, l, _ = proj3.shape
    assert b == SUBLANES, "the scan keeps the batch on the sublane axis"
    n_blocks, cw, bw = wbd.shape
    sw = n_blocks * cw
    tc = S5_TC
    rows = tc * b
    r = jnp.arange(rows)
    src = (r % b) * tc + r // b
    pm = (src[:, None] == jnp.arange(rows)[None, :]).astype(BF16)
    pmt = pm.T
    const = lambda shape: pl.BlockSpec(shape, lambda c: (0,) * len(shape))
    return pl.pallas_call(
        functools.partial(_s5_kernel, n_blocks=n_blocks, tc=tc, batch=b),
        grid=(l // tc,),
        in_specs=[
            pl.BlockSpec((b, tc, sw), lambda c: (0, c, u_block_index)),
            const((rows, rows)), const((rows, rows)),
            const((n_blocks, cw, bw)), const((n_blocks, bw, cw)),
            const((1, n_blocks * bw)), const((1, sw)),
            const((sw, sw)), const((1, sw)), const((1, sw)),
        ],
        out_specs=pl.BlockSpec((b, tc, sw), lambda c: (0, c, 0)),
        out_shape=jax.ShapeDtypeStruct((b, l, sw), BF16),
        scratch_shapes=[pltpu.VMEM((rows, n_blocks * bw), F32),
                        pltpu.VMEM((b, n_blocks * bw), F32)],
        compiler_params=_cparams(("arbitrary",)),
        name="s5",
    )(proj3, pm, pmt, wbd, cbd, avec, d.reshape(1, sw), wglu_bf16, bglu.reshape(1, sw),
      g.reshape(1, sw))


def _outproj_kernel(ma_ref, ms_ref, x_ref, wo_ref, g2_ref, wr_ref, br_ref, tri_ref,
                    h_ref, n2s_ref, ri_ref, rf_ref, cnt_ref, carry_ref, n2_ref):
    aw = ma_ref.shape[-1]

    @pl.when(pl.program_id(0) == 0)
    def _():
        carry_ref[...] = jnp.zeros_like(carry_ref)

    h = (x_ref[...]
         + jnp.dot(ma_ref[...], wo_ref[:aw, :], preferred_element_type=F32)
         + jnp.dot(ms_ref[...], wo_ref[aw:, :], preferred_element_type=F32))
    h_ref[...] = h
    n2 = _rms(h, g2_ref[...])
    n2_ref[...] = n2
    _to_slab(n2s_ref, 0, n2_ref, 0, n2.shape[0] // SUBLANES)

    n_hi = n2.astype(BF16)
    n_lo = (n2 - n_hi.astype(F32)).astype(BF16)
    w = wr_ref[...]
    w_hi = w.astype(BF16)
    w_lo = (w - w_hi.astype(F32)).astype(BF16)
    logits = (jnp.dot(n_hi, w_hi, preferred_element_type=F32)
              + jnp.dot(n_lo, w_hi, preferred_element_type=F32)
              + jnp.dot(n_hi, w_lo, preferred_element_type=F32)) + br_ref[...]

    tm, ne = logits.shape
    lane = lax.broadcasted_iota(I32, (tm, ne), 1)
    work = logits
    vals, idxs = [], []
    for _ in range(TOP_K):
        m = jnp.max(work, axis=-1, keepdims=True)
        ik = jnp.min(jnp.where(work == m, lane, ne), axis=-1, keepdims=True)
        vals.append(m)
        idxs.append(ik)
        work = jnp.where(lane == ik, -jnp.inf, work)
    es = [jnp.exp(v - vals[0]) for v in vals]
    den = es[0] + es[1] + es[2] + es[3]

    hot = jnp.zeros((tm, ne), F32)
    for ik in idxs:
        hot = hot + (lane == ik).astype(F32)
    before = jnp.dot(tri_ref[...], hot.astype(BF16), preferred_element_type=F32) + carry_ref[...]
    ri = jnp.zeros((tm, ne), I32)
    rf = jnp.zeros((tm, ne), F32)
    for k in range(TOP_K):
        rank = jnp.sum(jnp.where(lane == idxs[k], before, 0.0), axis=-1, keepdims=True)
        ri = jnp.where(lane == k, idxs[k], ri)
        ri = jnp.where(lane == TOP_K + k, rank.astype(I32), ri)
        rf = jnp.where(lane == k, es[k] / den, rf)
    ri_ref[...] = ri
    rf_ref[...] = rf
    carry_ref[...] = carry_ref[...] + jnp.sum(hot, axis=0, keepdims=True)
    cnt_ref[...] = carry_ref[...]


def _outproj(ma, ms, x2d, wo_bf16, g2, w_router, b_router, tm):
    t, d = x2d.shape
    aw = ma.shape[1]
    ne = w_router.shape[1]
    pieces = d // LANES
    wr = jnp.zeros((d, LANES), F32).at[:, :ne].set(w_router.astype(F32))
    br = jnp.full((1, LANES), NEG_BIG, F32).at[0, :ne].set(b_router.astype(F32))
    tri = (jnp.arange(tm)[:, None] > jnp.arange(tm)[None, :]).astype(BF16)
    const = lambda shape: pl.BlockSpec(shape, lambda i: (0,) * len(shape))
    row = lambda w: pl.BlockSpec((tm, w), lambda i: (i, 0))
    return pl.pallas_call(
        _outproj_kernel,
        grid=(t // tm,),
        in_specs=[row(aw), row(ms.shape[1]), row(d), const((d, d)), const((1, d)),
                  const((d, LANES)), const((1, LANES)), const((tm, tm))],
        out_specs=[row(d), pl.BlockSpec((tm * pieces, LANES), lambda i: (i, 0)),
                   row(LANES), row(LANES), const((1, LANES))],
        out_shape=[jax.ShapeDtypeStruct((t, d), F32), jax.ShapeDtypeStruct((t * pieces, LANES), F32),
                   jax.ShapeDtypeStruct((t, LANES), I32), jax.ShapeDtypeStruct((t, LANES), F32),
                   jax.ShapeDtypeStruct((1, LANES), F32)],
        scratch_shapes=[pltpu.VMEM((1, LANES), F32), pltpu.VMEM((tm, d), F32)],
        compiler_params=_cparams(("arbitrary",)),
        name="outproj",
    )(ma, ms, x2d, wo_bf16, g2.reshape(1, d), wr, br, tri)


def _route_tables(route_i, counts, n_tokens, n_experts):
    idx = route_i[:, :TOP_K]
    rank = route_i[:, TOP_K:2 * TOP_K]
    n_tiles_max = n_tokens * TOP_K // MOE_TM + n_experts
    s_max = n_experts + n_tiles_max // MOE_NTS
    e_iota = jnp.arange(n_experts, dtype=I32)

    tiles_e = (counts + MOE_TM - 1) // MOE_TM
    tile_end = jnp.cumsum(tiles_e)
    tile_off = tile_end - tiles_e
    off_of = jnp.sum(jnp.where(idx[..., None] == e_iota, tile_off, 0), axis=-1)
    pos = (off_of * MOE_TM + rank).astype(I32)

    nst_e = (tiles_e + MOE_NTS - 1) // MOE_NTS
    st_incl = jnp.cumsum(nst_e)
    num_st = st_incl[-1]
    s = jnp.arange(s_max, dtype=I32)
    active = s < num_st
    s_eff = jnp.minimum(s, num_st - 1)
    e_s = jnp.minimum(jnp.sum(st_incl[None, :] <= s_eff[:, None], axis=1), n_experts - 1).astype(I32)
    onehot = e_s[:, None] == e_iota[None, :]
    pick = lambda v: jnp.sum(jnp.where(onehot, v[None, :], 0), axis=1)
    local = s_eff - (pick(st_incl) - pick(nst_e))
    tile0 = (pick(tile_off) + MOE_NTS * local).astype(I32)
    nt = jnp.where(active, jnp.minimum(MOE_NTS, pick(tiles_e) - MOE_NTS * local), 0).astype(I32)

    tile_ids = jnp.arange(n_tiles_max, dtype=I32)
    is_last = jnp.any((tile_ids[:, None] == tile_end[None, :] - 1) & (tiles_e[None, :] > 0), axis=1)
    pad_tile = (is_last | (tile_ids >= tile_end[-1])).astype(I32)
    return pos, e_s, tile0, nt, pad_tile, tile_end[-1:].astype(I32)


def _dispatch_kernel(pad_tile, pos_ref, n2_ref, xs_hbm, zero_ref, sem_z, sem_r, *,
                     tokens_per_step, pieces):
    i = pl.program_id(0)
    tile_rows = MOE_TM * pieces
    n_tiles = pad_tile.shape[0]

    @pl.when(i == 0)
    def _zero_fill():
        zero_ref[...] = jnp.zeros_like(zero_ref)

        def fill(tl):
            start = pl.multiple_of(tl * tile_rows, tile_rows)
            return pltpu.make_async_copy(zero_ref, xs_hbm.at[pl.ds(start, tile_rows)], sem_z)

        def issue(tl, c):
            @pl.when(pad_tile[tl] > 0)
            def _():
                fill(tl).start()
            return c

        def drain(tl, c):
            @pl.when(pad_tile[tl] > 0)
            def _():
                fill(tl).wait()
            return c

        lax.fori_loop(0, n_tiles, issue, 0)
        lax.fori_loop(0, n_tiles, drain, 0)

    def row_copy(tok, dst):
        return pltpu.make_async_copy(
            n2_ref.at[pl.ds(pl.multiple_of(tok * pieces, pieces), pieces)],
            xs_hbm.at[pl.ds(pl.multiple_of(dst * pieces, pieces), pieces)], sem_r)

    def token(tok, c):
        for k in range(TOP_K):
            row_copy(tok, pos_ref[tok * TOP_K + k]).start()
        return c

    lax.fori_loop(0, tokens_per_step, token, 0, unroll=4)

    def drain(m, c):
        for _ in range(LANES):
            row_copy(0, 0).wait()
        return c

    lax.fori_loop(0, tokens_per_step * TOP_K // LANES, drain, 0)


def _dispatch(n2_slab, pos, pad_tile, d):
    pieces = d // LANES
    n_tokens = n2_slab.shape[0] // pieces
    n_tiles = pad_tile.shape[0]
    tm = min(DISPATCH_TM, n_tokens)
    grid_spec = pltpu.PrefetchScalarGridSpec(
        num_scalar_prefetch=1,
        grid=(n_tokens // tm,),
        in_specs=[pl.BlockSpec((tm * TOP_K,), lambda i, pad: (i,), memory_space=pltpu.SMEM),
                  pl.BlockSpec((tm * pieces, LANES), lambda i, pad: (i, 0))],
        out_specs=pl.BlockSpec(memory_space=pl.ANY),
        scratch_shapes=[pltpu.VMEM((MOE_TM * pieces, LANES), F32),
                        pltpu.SemaphoreType.DMA(()), pltpu.SemaphoreType.DMA(())],
    )
    return pl.pallas_call(
        functools.partial(_dispatch_kernel, tokens_per_step=tm, pieces=pieces),
        grid_spec=grid_spec,
        out_shape=jax.ShapeDtypeStruct((n_tiles * MOE_TM * pieces, LANES), F32),
        compiler_params=_cparams(("arbitrary",)),
        name="dispatch",
    )(pad_tile, pos.reshape(-1), n2_slab)


def _experts_kernel(st_e, st_t0, st_nt, n_used, xs_hbm, wg_ref, wu_ref, wd_ref, bg_ref, bu_ref, bd_ref,
                    y_hbm, tbuf, xs_b, acc, wgb, wub, wdb, sem_g, sem_s, *, n_j, n_tiles):
    s = pl.program_id(0)
    j = pl.program_id(1)
    nt = st_nt[s]
    t0 = st_t0[s]
    pieces = acc.shape[-1] // LANES
    tile_rows = MOE_TM * pieces

    def hbm_tile(ref, tl):
        return ref.at[pl.ds(pl.multiple_of(tl * tile_rows, tile_rows), tile_rows)]

    def tile_in(q):
        return pltpu.make_async_copy(hbm_tile(xs_hbm, t0 + q), hbm_tile(tbuf, q & 1), sem_g.at[q & 1])

    def tile_out(q, tl):
        return pltpu.make_async_copy(hbm_tile(tbuf, q & 1), hbm_tile(y_hbm, tl), sem_s.at[q & 1])

    def for_tiles(lo, hi, fn):
        def body(q, c):
            fn(q)
            return c
        lax.fori_loop(lo, hi, body, 0)

    @pl.when((s == 0) & (j == 0))
    def _zero_tail():
        tbuf[pl.ds(0, tile_rows), :] = jnp.zeros((tile_rows, LANES), F32)
        for_tiles(n_used[0], n_tiles, lambda tl: tile_out(0, tl).start())
        for_tiles(n_used[0], n_tiles, lambda tl: tile_out(0, tl).wait())

    @pl.when(nt > 0)
    def _active():
        @pl.when(j == 0)
        def _load():
            tile_in(0).start()

            def load(q):
                @pl.when(q + 1 < nt)
                def _():
                    tile_in(q + 1).start()

                tile_in(q).wait()
                _from_slab(xs_b, q * MOE_TM, tbuf, (q & 1) * MOE_TM, MOE_TM // (2 * SUBLANES))
                acc[pl.ds(pl.multiple_of(q * MOE_TM, MOE_TM), MOE_TM), :] = jnp.broadcast_to(
                    bd_ref[...], (MOE_TM, acc.shape[-1]))

            for_tiles(0, nt, load)

        wgb[...] = wg_ref[...].astype(BF16)
        wub[...] = wu_ref[...].astype(BF16)
        wdb[...] = wd_ref[...].astype(BF16)

        def ffn(tile0, n_tiles_chunk):
            rows = n_tiles_chunk * MOE_TM
            r0 = pl.multiple_of(tile0 * MOE_TM, MOE_TM)
            xt = xs_b[pl.ds(r0, rows), :]
            g = jnp.dot(xt, wgb[...], preferred_element_type=F32) + bg_ref[...]
            u = jnp.dot(xt, wub[...], preferred_element_type=F32) + bu_ref[...]
            g = jnp.minimum(g, SWIGLU_LIMIT)
            u = jnp.clip(u, -SWIGLU_LIMIT, SWIGLU_LIMIT)
            hdn = g * jax.nn.sigmoid(SWIGLU_ALPHA * g) * (u + 1.0)
            acc[pl.ds(r0, rows), :] += jnp.dot(hdn.astype(BF16), wdb[...], preferred_element_type=F32)

        big = MOE_CHUNK_TILES
        for_tiles(0, nt // big, lambda c: ffn(c * big, big))
        done = (nt // big) * big
        piece = big // 2
        while piece >= 1:
            @pl.when((nt & piece) != 0)
            def _(done=done, piece=piece):
                ffn(done, piece)

            done = done + (nt & piece)
            piece //= 2

        @pl.when(j == n_j - 1)
        def _store():
            def store(q):
                @pl.when(q >= 2)
                def _():
                    tile_out(q - 2, t0 + q - 2).wait()

                _to_slab(tbuf, (q & 1) * MOE_TM, acc, q * MOE_TM, MOE_TM // SUBLANES)
                tile_out(q, t0 + q).start()

            for_tiles(0, nt, store)

            @pl.when(nt >= 2)
            def _():
                tile_out(nt - 2, t0 + nt - 2).wait()

            tile_out(nt - 1, t0 + nt - 1).wait()


def _experts(xs_slab, tables, w_gate, b_gate, w_up, b_up, w_down, b_down):
    st_e, st_t0, st_nt, n_used = tables
    ne, d, f = w_gate.shape
    pieces = d // LANES
    n_tiles = xs_slab.shape[0] // (MOE_TM * pieces)
    n_j = f // MOE_TF
    s_max = st_e.shape[0]
    super_rows = MOE_NTS * MOE_TM

    def w_in_map(s, j, st_e, st_t0, st_nt, n_used):
        return (st_e[s], 0, jnp.where(st_nt[s] > 0, j, n_j - 1))

    def w_down_map(s, j, st_e, st_t0, st_nt, n_used):
        return (st_e[s], jnp.where(st_nt[s] > 0, j, n_j - 1), 0)

    def b_down_map(s, j, st_e, st_t0, st_nt, n_used):
        return (st_e[s], 0, 0)

    any_spec = pl.BlockSpec(memory_space=pl.ANY)
    grid_spec = pltpu.PrefetchScalarGridSpec(
        num_scalar_prefetch=4,
        grid=(s_max, n_j),
        in_specs=[
            any_spec,
            pl.BlockSpec((None, d, MOE_TF), w_in_map),
            pl.BlockSpec((None, d, MOE_TF), w_in_map),
            pl.BlockSpec((None, MOE_TF, d), w_down_map),
            pl.BlockSpec((None, 1, MOE_TF), w_in_map),
            pl.BlockSpec((None, 1, MOE_TF), w_in_map),
            pl.BlockSpec((None, 1, d), b_down_map),
        ],
        out_specs=any_spec,
        scratch_shapes=[
            pltpu.VMEM((2 * MOE_TM * pieces, LANES), F32),
            pltpu.VMEM((super_rows, d), BF16),
            pltpu.VMEM((super_rows, d), F32),
            pltpu.VMEM((d, MOE_TF), BF16),
            pltpu.VMEM((d, MOE_TF), BF16),
            pltpu.VMEM((MOE_TF, d), BF16),
            pltpu.SemaphoreType.DMA((2,)),
            pltpu.SemaphoreType.DMA((2,)),
        ],
    )
    return pl.pallas_call(
        functools.partial(_experts_kernel, n_j=n_j, n_tiles=n_tiles),
        grid_spec=grid_spec,
        out_shape=jax.ShapeDtypeStruct(xs_slab.shape, F32),
        compiler_params=_cparams(("arbitrary", "arbitrary")),
        name="experts",
    )(st_e, st_t0, st_nt, n_used, xs_slab,
      w_gate, w_up, w_down, b_gate.reshape(ne, 1, f), b_up.reshape(ne, 1, f),
      b_down.reshape(ne, 1, d))


def _combine_kernel(pos_ref, pos_next_ref, h_ref, rf_ref, g_ref, y_hbm, o_ref, yv, acc_ref, sem, *,
                    final_norm):
    i = pl.program_id(0)
    tm, d = h_ref.shape
    pieces = d // LANES
    slot_rows = TOP_K * tm
    slot = i & 1

    def row_copy(src, dst, sl):
        return pltpu.make_async_copy(
            y_hbm.at[pl.ds(pl.multiple_of(src * pieces, pieces), pieces)],
            yv.at[pl.ds(pl.multiple_of((sl * slot_rows + dst) * pieces, pieces), pieces)], sem.at[sl])

    def gather(table, sl):
        def token(tok, c):
            for k in range(TOP_K):
                row_copy(table[tok * TOP_K + k], k * tm + tok, sl).start()
            return c

        lax.fori_loop(0, tm, token, 0, unroll=4)

    @pl.when(i == 0)
    def _():
        gather(pos_ref, 0)

    @pl.when(i + 1 < pl.num_programs(0))
    def _():
        gather(pos_next_ref, 1 - slot)

    def drain(m, c):
        for _ in range(LANES):
            row_copy(0, 0, slot).wait()
        return c

    lax.fori_loop(0, slot_rows // LANES, drain, 0)

    def group(g, c):
        r0 = pl.multiple_of(g * SUBLANES, SUBLANES)
        rf = rf_ref[pl.ds(r0, SUBLANES), :]
        gates = [jnp.broadcast_to(rf[:, k:k + 1], (SUBLANES, LANES)) for k in range(TOP_K)]
        for p in range(pieces):
            v = h_ref[pl.ds(r0, SUBLANES), p * LANES:(p + 1) * LANES]
            for k in range(TOP_K):
                row = (slot * slot_rows + k * tm + r0) * pieces + p
                v = v + gates[k] * yv[pl.ds(row, SUBLANES, stride=pieces), :]
            acc_ref[pl.ds(r0, SUBLANES), p * LANES:(p + 1) * LANES] = v
        return c

    lax.fori_loop(0, tm // SUBLANES, group, 0)
    o_ref[...] = _rms(acc_ref[...], g_ref[...]) if final_norm else acc_ref[...]


def _combine(h, y_slab, pos, route_f, g, final_norm):
    t, d = h.shape
    pieces = d // LANES
    tm = min(COMBINE_TM, t)
    n_steps = t // tm
    table = lambda index: pl.BlockSpec((tm * TOP_K,), index, memory_space=pltpu.SMEM)
    return pl.pallas_call(
        functools.partial(_combine_kernel, final_norm=final_norm),
        grid=(n_steps,),
        in_specs=[table(lambda i: (i,)),
                  table(lambda i: (jnp.minimum(i + 1, n_steps - 1),)),
                  pl.BlockSpec((tm, d), lambda i: (i, 0)),
                  pl.BlockSpec((tm, LANES), lambda i: (i, 0)),
                  pl.BlockSpec((1, d), lambda i: (0, 0)),
                  pl.BlockSpec(memory_space=pl.ANY)],
        out_specs=pl.BlockSpec((tm, d), lambda i: (i, 0)),
        out_shape=jax.ShapeDtypeStruct((t, d), F32),
        scratch_shapes=[pltpu.VMEM((2 * TOP_K * tm * pieces, LANES), F32), pltpu.VMEM((tm, d), F32),
                        pltpu.SemaphoreType.DMA((2,))],
        compiler_params=_cparams(("arbitrary",)),
        name="combine",
    )(pos.reshape(-1), pos.reshape(-1), h, route_f, g.reshape(1, d), y_slab)


def kernel(x, norm1_g, w_in, rel_bias, ssm_lambda_re, ssm_lambda_im, ssm_log_step, ssm_b_re, ssm_b_im, ssm_c_re, ssm_c_im, ssm_d, w_ssm_glu, b_ssm_glu, attn_out_g, ssm_out_g, w_out, norm2_g, w_router, b_router, w_gate, b_gate, w_up, b_up, w_down, b_down, norm_f_g):
    b, l, d = x.shape
    depth = w_in.shape[0]
    aw = attn_out_g.shape[-1]
    n_heads = aw // HEAD_DIM
    n_experts = w_router.shape[-1]
    t = b * l
    assert l % ATT_TQ == 0 and l >= ATT_WIN and l % S5_TC == 0
    assert (t * TOP_K) % (MOE_NTS * MOE_TM) == 0 and t % DISPATCH_TM == 0

    h = x.reshape(t, d).astype(F32)
    for li in range(depth):
        last = li == depth - 1
        proj = _inproj(h, norm1_g[li], w_in[li].astype(BF16), tm=min(1024, t), tn=512)
        proj3 = proj.reshape(b, l, -1)
        mixed_a = _attention(proj3, _attn_bias_table(rel_bias[li]), attn_out_g[li], n_heads)
        wbd, cbd, avec = _s5_params(ssm_lambda_re[li], ssm_lambda_im[li], ssm_log_step[li],
                                    ssm_b_re[li], ssm_b_im[li], ssm_c_re[li], ssm_c_im[li])
        mixed_s = _s5(proj3, wbd, cbd, avec, ssm_d[li], w_ssm_glu[li].astype(BF16), b_ssm_glu[li],
                      ssm_out_g[li], u_block_index=3 * aw // (d - aw))
        h, n2_slab, route_i, route_f, cnt = _outproj(
            mixed_a.reshape(t, aw), mixed_s.reshape(t, d - aw), h, w_out[li].astype(BF16),
            norm2_g[li], w_router[li], b_router[li], tm=min(512, t))
        pos, st_e, st_t0, st_nt, pad_tile, n_used = _route_tables(
            route_i, cnt[0, :n_experts].astype(I32), t, n_experts)
        xs_slab = _dispatch(n2_slab, pos, pad_tile, d)
        ys_slab = _experts(xs_slab, (st_e, st_t0, st_nt, n_used), w_gate[li], b_gate[li], w_up[li],
                           b_up[li], w_down[li], b_down[li])
        g_fin = norm_f_g if last else jnp.ones((d,), F32)
        h = _combine(h, ys_slab, pos, route_f, g_fin, final_norm=last)
    return h.reshape(b, l, d).astype(x.dtype)
```

```python
import functools
import math

import jax
import jax.numpy as jnp
from jax import lax
from jax.experimental import pallas as pl
from jax.experimental.pallas import tpu as pltpu

F32 = jnp.float32
BF16 = jnp.bfloat16
I32 = jnp.int32

CHUNK = 64
HEAD_DIM = 128
N_BACK_CHUNKS = 8
REL_CLIP = 128
SSM_GROUP = 16
SSM_STATE = 64
TOP_K = 4
SWIGLU_LIMIT = 7.0
SWIGLU_ALPHA = 1.702
EPS = 1e-5
NEG_BIG = -1e30

LANES = 128
SUBLANES = 8
MXU_DIM = 256
VMEM_LIMIT_BYTES = 56 * 1024 * 1024

ATT_TQ = 2 * CHUNK
ATT_WIN = (N_BACK_CHUNKS + 2) * CHUNK
ATT_NSHIFT = (N_BACK_CHUNKS * CHUNK) // ATT_TQ + 1

S5_GROUPS_PER_BLOCK = MXU_DIM // SSM_GROUP
S5_BLOCK_STATES = S5_GROUPS_PER_BLOCK * SSM_STATE
S5_STRIP = 512
S5_TC = 32

MOE_TM = 256
MOE_NTS = 9
MOE_TF = 256
MOE_CHUNK_TILES = 4
DISPATCH_TM = 512
COMBINE_TM = 256


def _cparams(sem):
    return pltpu.CompilerParams(dimension_semantics=sem, vmem_limit_bytes=VMEM_LIMIT_BYTES)


def _rms(x, g):
    r = lax.rsqrt(jnp.mean(x * x, axis=-1, keepdims=True) + EPS)
    return x * r * g


def _to_slab(slab_ref, slab_row0, src_ref, src_row0, n_groups):
    pieces = src_ref.shape[-1] // LANES

    def group(g, c):
        r0 = g * SUBLANES
        src0 = pl.multiple_of(src_row0 + r0, SUBLANES)
        for p in range(pieces):
            slab_ref[pl.ds((slab_row0 + r0) * pieces + p, SUBLANES, stride=pieces), :] = (
                src_ref[pl.ds(src0, SUBLANES), p * LANES:(p + 1) * LANES])
        return c

    lax.fori_loop(0, n_groups, group, 0)


def _from_slab(dst_ref, dst_row0, slab_ref, slab_row0, n_groups):
    pieces = dst_ref.shape[-1] // LANES
    rows = 2 * SUBLANES

    def group(g, c):
        r0 = g * rows
        dst0 = pl.multiple_of(dst_row0 + r0, rows)
        for p in range(pieces):
            lo = slab_ref[pl.ds((slab_row0 + r0) * pieces + p, SUBLANES, stride=pieces), :]
            hi = slab_ref[pl.ds((slab_row0 + r0 + SUBLANES) * pieces + p, SUBLANES, stride=pieces), :]
            dst_ref[pl.ds(dst0, rows), p * LANES:(p + 1) * LANES] = (
                jnp.concatenate([lo, hi], axis=0).astype(dst_ref.dtype))
        return c

    lax.fori_loop(0, n_groups, group, 0)


def _inproj_kernel(x_ref, g_ref, w_ref, o_ref, nb_ref):
    @pl.when(pl.program_id(1) == 0)
    def _():
        nb_ref[...] = _rms(x_ref[...], g_ref[...]).astype(BF16)

    o_ref[...] = jnp.dot(nb_ref[...], w_ref[...], preferred_element_type=F32).astype(o_ref.dtype)


def _inproj(x2d, g, w_bf16, tm, tn):
    t, d = x2d.shape
    n = w_bf16.shape[1]
    return pl.pallas_call(
        _inproj_kernel,
        grid=(t // tm, n // tn),
        in_specs=[
            pl.BlockSpec((tm, d), lambda i, j: (i, 0)),
            pl.BlockSpec((1, d), lambda i, j: (0, 0)),
            pl.BlockSpec((d, tn), lambda i, j: (0, j)),
        ],
        out_specs=pl.BlockSpec((tm, tn), lambda i, j: (i, j)),
        out_shape=jax.ShapeDtypeStruct((t, n), BF16),
        scratch_shapes=[pltpu.VMEM((tm, d), BF16)],
        compiler_params=_cparams(("arbitrary", "arbitrary")),
        name="inproj",
    )(x2d, g.reshape(1, d), w_bf16)


def _attn_bias_table(rel_bias):
    n_heads = rel_bias.shape[0]
    period = ATT_WIN + ATT_TQ
    m = jnp.arange(period)
    diff = jnp.where(m < ATT_WIN, m, m - period)
    sh = jnp.arange(ATT_NSHIFT)[:, None]
    rel = N_BACK_CHUNKS * CHUNK - ATT_TQ * sh - diff[None, :]
    idx = jnp.clip(rel, -REL_CLIP, REL_CLIP) + REL_CLIP
    vext = rel_bias.astype(F32)[:, idx]
    flat = jnp.tile(vext, (1, 1, ATT_TQ))[..., :ATT_TQ * (period - 1)]
    tbl = flat.reshape(n_heads, ATT_NSHIFT, ATT_TQ, period - 1)[..., :ATT_WIN]
    return jnp.transpose(tbl, (1, 0, 2, 3))


def _attn_kernel(q_ref, k_ref, v_ref, b_ref, g_ref, o_ref, a_ref, *, n_heads):
    i = pl.program_id(1)
    s0 = pl.multiple_of(jnp.maximum(i * ATT_TQ - N_BACK_CHUNKS * CHUNK, 0), ATT_TQ)
    row = lax.broadcasted_iota(I32, (ATT_TQ, ATT_WIN), 0)
    col = lax.broadcasted_iota(I32, (ATT_TQ, ATT_WIN), 1)
    qc = (i * ATT_TQ + row) // CHUNK
    kc = (s0 + col) // CHUNK
    ok = (kc <= qc) & (kc >= qc - N_BACK_CHUNKS)
    scale = 1.0 / math.sqrt(HEAD_DIM)
    for h in range(n_heads):
        hs = slice(h * HEAD_DIM, (h + 1) * HEAD_DIM)
        qh = q_ref[:, hs]
        kh = k_ref[pl.ds(s0, ATT_WIN), hs]
        vh = v_ref[pl.ds(s0, ATT_WIN), hs]
        s = lax.dot_general(qh, kh, (((1,), (1,)), ((), ())), preferred_element_type=F32)
        s = jnp.where(ok, s * scale + b_ref[h], NEG_BIG)
        m = jnp.max(s, axis=-1, keepdims=True)
        p = jnp.exp(s - m)
        l = jnp.sum(p, axis=-1, keepdims=True)
        o = jnp.dot(p.astype(BF16), vh, preferred_element_type=F32)
        a_ref[:, hs] = o / l
    o_ref[...] = _rms(a_ref[...], g_ref[...]).astype(o_ref.dtype)


def _attention(proj3, bias_tbl, g, n_heads):
    b, l, _ = proj3.shape
    aw = n_heads * HEAD_DIM
    nsh = ATT_NSHIFT
    return pl.pallas_call(
        functools.partial(_attn_kernel, n_heads=n_heads),
        grid=(b, l // ATT_TQ),
        in_specs=[
            pl.BlockSpec((None, ATT_TQ, aw), lambda bi, i: (bi, i, 0)),
            pl.BlockSpec((None, l, aw), lambda bi, i: (bi, 0, 1)),
            pl.BlockSpec((None, l, aw), lambda bi, i: (bi, 0, 2)),
            pl.BlockSpec((None, n_heads, ATT_TQ, ATT_WIN),
                         lambda bi, i: (jnp.maximum(nsh - 1 - i, 0), 0, 0, 0)),
            pl.BlockSpec((1, aw), lambda bi, i: (0, 0)),
        ],
        out_specs=pl.BlockSpec((None, ATT_TQ, aw), lambda bi, i: (bi, i, 0)),
        out_shape=jax.ShapeDtypeStruct((b, l, aw), BF16),
        scratch_shapes=[pltpu.VMEM((ATT_TQ, aw), F32)],
        compiler_params=_cparams(("arbitrary", "arbitrary")),
        name="attn",
    )(proj3, proj3, proj3, bias_tbl, g.reshape(1, aw))


def _s5_params(lam_re, lam_im, log_step, b_re, b_im, c_re, c_im):
    g, p = lam_re.shape
    nb = g // S5_GROUPS_PER_BLOCK
    gl = S5_GROUPS_PER_BLOCK
    dt = jnp.exp(log_step.astype(F32))[:, None]
    lr = lam_re.astype(F32)
    li = lam_im.astype(F32)
    mag = jnp.exp(lr * dt)
    ab_re = mag * jnp.cos(li * dt)
    ab_im = mag * jnp.sin(li * dt)
    den = lr * lr + li * li
    nr = ab_re - 1.0
    ni = ab_im
    f_re = (nr * lr + ni * li) / den
    f_im = (ni * lr - nr * li) / den
    br = b_re.astype(F32)
    bi = b_im.astype(F32)
    bb_re = f_re[..., None] * br - f_im[..., None] * bi
    bb_im = f_re[..., None] * bi + f_im[..., None] * br
    eye = jnp.eye(gl, dtype=F32)
    bb = jnp.stack([bb_re, bb_im]).reshape(2, nb, gl, p, SSM_GROUP)
    wbd = jnp.einsum("ajgpc,gh->jgcahp", bb, eye).reshape(nb, gl * SSM_GROUP, 2 * gl * p)
    cc = jnp.stack([c_re.astype(F32), -c_im.astype(F32)]).reshape(2, nb, gl, SSM_GROUP, p)
    cbd = jnp.einsum("ajgcp,gh->jagphc", cc, eye).reshape(nb, 2 * gl * p, gl * SSM_GROUP)
    avec = jnp.stack([ab_re, ab_im]).reshape(2, nb, gl, p)
    avec = jnp.transpose(avec, (1, 0, 2, 3)).reshape(1, 2 * g * p)
    return wbd.astype(BF16), cbd.astype(BF16), avec


def _s5_kernel(u_ref, pm_ref, pmt_ref, wbd_ref, cbd_ref, av_ref, d_ref, wglu_ref, bglu_ref,
               g_ref, o_ref, xs_ref, hst_ref, *, n_blocks, tc, batch):
    rows = tc * batch
    sw = u_ref.shape[-1]
    bw = 2 * S5_BLOCK_STATES
    cw = S5_GROUPS_PER_BLOCK * SSM_GROUP

    @pl.when(pl.program_id(0) == 0)
    def _():
        hst_ref[...] = jnp.zeros_like(hst_ref)

    u_tm = jnp.dot(pm_ref[...], u_ref[...].reshape(rows, sw), preferred_element_type=F32)
    u_tmb = u_tm.astype(BF16)
    for j in range(n_blocks):
        xs_ref[:, j * bw:(j + 1) * bw] = jnp.dot(
            u_tmb[:, j * cw:(j + 1) * cw], wbd_ref[j], preferred_element_type=F32)

    for j in range(n_blocks):
        for s in range(S5_BLOCK_STATES // S5_STRIP):
            cr = j * bw + s * S5_STRIP
            ci = cr + S5_BLOCK_STATES
            a_re = jnp.broadcast_to(av_ref[:, cr:cr + S5_STRIP], (batch, S5_STRIP))
            a_im = jnp.broadcast_to(av_ref[:, ci:ci + S5_STRIP], (batch, S5_STRIP))

            def step(t, carry, cr=cr, ci=ci, a_re=a_re, a_im=a_im):
                h_re, h_im = carry
                r0 = pl.multiple_of(t * batch, batch)
                n_re = a_re * h_re - a_im * h_im + xs_ref[pl.ds(r0, batch), cr:cr + S5_STRIP]
                n_im = a_re * h_im + a_im * h_re + xs_ref[pl.ds(r0, batch), ci:ci + S5_STRIP]
                xs_ref[pl.ds(r0, batch), cr:cr + S5_STRIP] = n_re
                xs_ref[pl.ds(r0, batch), ci:ci + S5_STRIP] = n_im
                return n_re, n_im

            h_re, h_im = lax.fori_loop(
                0, tc, step, (hst_ref[:, cr:cr + S5_STRIP], hst_ref[:, ci:ci + S5_STRIP]), unroll=4)
            hst_ref[:, cr:cr + S5_STRIP] = h_re
            hst_ref[:, ci:ci + S5_STRIP] = h_im

    ys = [jnp.dot(xs_ref[:, j * bw:(j + 1) * bw].astype(BF16), cbd_ref[j], preferred_element_type=F32)
          for j in range(n_blocks)]
    y = jnp.concatenate(ys, axis=1) + d_ref[...] * u_tm
    y = jax.nn.gelu(y)
    z = jnp.dot(y.astype(BF16), wglu_ref[...], preferred_element_type=F32) + bglu_ref[...]
    y = y * jax.nn.sigmoid(z)
    res = _rms(y, g_ref[...]).astype(BF16)
    out_bm = jnp.dot(pmt_ref[...], res, preferred_element_type=F32).astype(o_ref.dtype)
    o_ref[...] = out_bm.reshape(batch, tc, sw)


def _s5(proj3, wbd, cbd, avec, d, wglu_bf16, bglu, g, u_block_index):
    b, l, _ = proj3.shape
    assert b == SUBLANES, "the scan keeps the batch on the sublane axis"
    n_blocks, cw, bw = wbd.shape
    sw = n_blocks * cw
    tc = S5_TC
    rows = tc * b
    r = jnp.arange(rows)
    src = (r % b) * tc + r // b
    pm = (src[:, None] == jnp.arange(rows)[None, :]).astype(BF16)
    pmt = pm.T
    const = lambda shape: pl.BlockSpec(shape, lambda c: (0,) * len(shape))
    return pl.pallas_call(
        functools.partial(_s5_kernel, n_blocks=n_blocks, tc=tc, batch=b),
        grid=(l // tc,),
        in_specs=[
            pl.BlockSpec((b, tc, sw), lambda c: (0, c, u_block_index)),
            const((rows, rows)), const((rows, rows)),
            const((n_blocks, cw, bw)), const((n_blocks, bw, cw)),
            const((1, n_blocks * bw)), const((1, sw)),
            const((sw, sw)), const((1, sw)), const((1, sw)),
        ],
        out_specs=pl.BlockSpec((b, tc, sw), lambda c: (0, c, 0)),
        out_shape=jax.ShapeDtypeStruct((b, l, sw), BF16),
        scratch_shapes=[pltpu.VMEM((rows, n_blocks * bw), F32),
                        pltpu.VMEM((b, n_blocks * bw), F32)],
        compiler_params=_cparams(("arbitrary",)),
        name="s5",
    )(proj3, pm, pmt, wbd, cbd, avec, d.reshape(1, sw), wglu_bf16, bglu.reshape(1, sw),
      g.reshape(1, sw))


def _outproj_kernel(ma_ref, ms_ref, x_ref, wo_ref, g2_ref, wr_ref, br_ref, tri_ref,
                    h_ref, n2s_ref, ri_ref, rf_ref, cnt_ref, carry_ref, n2_ref):
    aw = ma_ref.shape[-1]

    @pl.when(pl.program_id(0) == 0)
    def _():
        carry_ref[...] = jnp.zeros_like(carry_ref)

    h = (x_ref[...]
         + jnp.dot(ma_ref[...], wo_ref[:aw, :], preferred_element_type=F32)
         + jnp.dot(ms_ref[...], wo_ref[aw:, :], preferred_element_type=F32))
    h_ref[...] = h
    n2 = _rms(h, g2_ref[...])
    n2_ref[...] = n2
    _to_slab(n2s_ref, 0, n2_ref, 0, n2.shape[0] // SUBLANES)

    n_hi = n2.astype(BF16)
    n_lo = (n2 - n_hi.astype(F32)).astype(BF16)
    w = wr_ref[...]
    w_hi = w.astype(BF16)
    w_lo = (w - w_hi.astype(F32)).astype(BF16)
    logits = (jnp.dot(n_hi, w_hi, preferred_element_type=F32)
              + jnp.dot(n_lo, w_hi, preferred_element_type=F32)
              + jnp.dot(n_hi, w_lo, preferred_element_type=F32)) + br_ref[...]

    tm, ne = logits.shape
    lane = lax.broadcasted_iota(I32, (tm, ne), 1)
    work = logits
    vals, idxs = [], []
    for _ in range(TOP_K):
        m = jnp.max(work, axis=-1, keepdims=True)
        ik = jnp.min(jnp.where(work == m, lane, ne), axis=-1, keepdims=True)
        vals.append(m)
        idxs.append(ik)
        work = jnp.where(lane == ik, -jnp.inf, work)
    es = [jnp.exp(v - vals[0]) for v in vals]
    den = es[0] + es[1] + es[2] + es[3]

    hot = jnp.zeros((tm, ne), F32)
    for ik in idxs:
        hot = hot + (lane == ik).astype(F32)
    before = jnp.dot(tri_ref[...], hot.astype(BF16), preferred_element_type=F32) + carry_ref[...]
    ri = jnp.zeros((tm, ne), I32)
    rf = jnp.zeros((tm, ne), F32)
    for k in range(TOP_K):
        rank = jnp.sum(jnp.where(lane == idxs[k], before, 0.0), axis=-1, keepdims=True)
        ri = jnp.where(lane == k, idxs[k], ri)
        ri = jnp.where(lane == TOP_K + k, rank.astype(I32), ri)
        rf = jnp.where(lane == k, es[k] / den, rf)
    ri_ref[...] = ri
    rf_ref[...] = rf
    carry_ref[...] = carry_ref[...] + jnp.sum(hot, axis=0, keepdims=True)
    cnt_ref[...] = carry_ref[...]


def _outproj(ma, ms, x2d, wo_bf16, g2, w_router, b_router, tm):
    t, d = x2d.shape
    aw = ma.shape[1]
    ne = w_router.shape[1]
    pieces = d // LANES
    wr = jnp.zeros((d, LANES), F32).at[:, :ne].set(w_router.astype(F32))
    br = jnp.full((1, LANES), NEG_BIG, F32).at[0, :ne].set(b_router.astype(F32))
    tri = (jnp.arange(tm)[:, None] > jnp.arange(tm)[None, :]).astype(BF16)
    const = lambda shape: pl.BlockSpec(shape, lambda i: (0,) * len(shape))
    row = lambda w: pl.BlockSpec((tm, w), lambda i: (i, 0))
    return pl.pallas_call(
        _outproj_kernel,
        grid=(t // tm,),
        in_specs=[row(aw), row(ms.shape[1]), row(d), const((d, d)), const((1, d)),
                  const((d, LANES)), const((1, LANES)), const((tm, tm))],
        out_specs=[row(d), pl.BlockSpec((tm * pieces, LANES), lambda i: (i, 0)),
                   row(LANES), row(LANES), const((1, LANES))],
        out_shape=[jax.ShapeDtypeStruct((t, d), F32), jax.ShapeDtypeStruct((t * pieces, LANES), F32),
                   jax.ShapeDtypeStruct((t, LANES), I32), jax.ShapeDtypeStruct((t, LANES), F32),
                   jax.ShapeDtypeStruct((1, LANES), F32)],
        scratch_shapes=[pltpu.VMEM((1, LANES), F32), pltpu.VMEM((tm, d), F32)],
        compiler_params=_cparams(("arbitrary",)),
        name="outproj",
    )(ma, ms, x2d, wo_bf16, g2.reshape(1, d), wr, br, tri)


def _route_tables(route_i, counts, n_tokens, n_experts):
    idx = route_i[:, :TOP_K]
    rank = route_i[:, TOP_K:2 * TOP_K]
    n_tiles_max = n_tokens * TOP_K // MOE_TM + n_experts
    s_max = n_experts + n_tiles_max // MOE_NTS
    e_iota = jnp.arange(n_experts, dtype=I32)

    tiles_e = (counts + MOE_TM - 1) // MOE_TM
    tile_end = jnp.cumsum(tiles_e)
    tile_off = tile_end - tiles_e
    off_of = jnp.sum(jnp.where(idx[..., None] == e_iota, tile_off, 0), axis=-1)
    pos = (off_of * MOE_TM + rank).astype(I32)

    nst_e = (tiles_e + MOE_NTS - 1) // MOE_NTS
    st_incl = jnp.cumsum(nst_e)
    num_st = st_incl[-1]
    s = jnp.arange(s_max, dtype=I32)
    active = s < num_st
    s_eff = jnp.minimum(s, num_st - 1)
    e_s = jnp.minimum(jnp.sum(st_incl[None, :] <= s_eff[:, None], axis=1), n_experts - 1).astype(I32)
    onehot = e_s[:, None] == e_iota[None, :]
    pick = lambda v: jnp.sum(jnp.where(onehot, v[None, :], 0), axis=1)
    local = s_eff - (pick(st_incl) - pick(nst_e))
    tile0 = (pick(tile_off) + MOE_NTS * local).astype(I32)
    nt = jnp.where(active, jnp.minimum(MOE_NTS, pick(tiles_e) - MOE_NTS * local), 0).astype(I32)

    tile_ids = jnp.arange(n_tiles_max, dtype=I32)
    is_last = jnp.any((tile_ids[:, None] == tile_end[None, :] - 1) & (tiles_e[None, :] > 0), axis=1)
    pad_tile = (is_last | (tile_ids >= tile_end[-1])).astype(I32)
    return pos, e_s, tile0, nt, pad_tile, tile_end[-1:].astype(I32), num_st.astype(I32)


def _dispatch_kernel(pad_tile, pos_ref, n2_ref, xs_hbm, zero_ref, sem_z, sem_r, *,
                     tokens_per_step, pieces):
    i = pl.program_id(0)
    tile_rows = MOE_TM * pieces
    n_tiles = pad_tile.shape[0]

    @pl.when(i == 0)
    def _zero_fill():
        zero_ref[...] = jnp.zeros_like(zero_ref)

        def fill(tl):
            start = pl.multiple_of(tl * tile_rows, tile_rows)
            return pltpu.make_async_copy(zero_ref, xs_hbm.at[pl.ds(start, tile_rows)], sem_z)

        def issue(tl, c):
            @pl.when(pad_tile[tl] > 0)
            def _():
                fill(tl).start()
            return c

        def drain(tl, c):
            @pl.when(pad_tile[tl] > 0)
            def _():
                fill(tl).wait()
            return c

        lax.fori_loop(0, n_tiles, issue, 0)
        lax.fori_loop(0, n_tiles, drain, 0)

    def row_copy(tok, dst):
        return pltpu.make_async_copy(
            n2_ref.at[pl.ds(pl.multiple_of(tok * pieces, pieces), pieces)],
            xs_hbm.at[pl.ds(pl.multiple_of(dst * pieces, pieces), pieces)], sem_r)

    def token(tok, c):
        for k in range(TOP_K):
            row_copy(tok, pos_ref[tok * TOP_K + k]).start(priority=k % 2)
        return c

    lax.fori_loop(0, tokens_per_step, token, 0, unroll=4)

    def drain(m, c):
        for _ in range(LANES):
            row_copy(0, 0).wait()
        return c

    lax.fori_loop(0, tokens_per_step * TOP_K // LANES, drain, 0)


def _dispatch(n2_slab, pos, pad_tile, d):
    pieces = d // LANES
    n_tokens = n2_slab.shape[0] // pieces
    n_tiles = pad_tile.shape[0]
    tm = min(DISPATCH_TM, n_tokens)
    grid_spec = pltpu.PrefetchScalarGridSpec(
        num_scalar_prefetch=1,
        grid=(n_tokens // tm,),
        in_specs=[pl.BlockSpec((tm * TOP_K,), lambda i, pad: (i,), memory_space=pltpu.SMEM),
                  pl.BlockSpec((tm * pieces, LANES), lambda i, pad: (i, 0))],
        out_specs=pl.BlockSpec(memory_space=pl.ANY),
        scratch_shapes=[pltpu.VMEM((MOE_TM * pieces, LANES), F32),
                        pltpu.SemaphoreType.DMA(()), pltpu.SemaphoreType.DMA(())],
    )
    return pl.pallas_call(
        functools.partial(_dispatch_kernel, tokens_per_step=tm, pieces=pieces),
        grid_spec=grid_spec,
        out_shape=jax.ShapeDtypeStruct((n_tiles * MOE_TM * pieces, LANES), F32),
        compiler_params=_cparams(("arbitrary",)),
        name="dispatch",
    )(pad_tile, pos.reshape(-1), n2_slab)


def _experts_kernel(st_e, st_t0, st_nt, n_used, xs_hbm, wg_ref, wu_ref, wd_ref, bg_ref, bu_ref, bd_ref,
                    y_hbm, tbuf, xs_b, acc, wgb, wub, wdb, sem_g, sem_s, *, n_j, n_tiles):
    s = pl.program_id(0)
    j = pl.program_id(1)
    nt = st_nt[s]
    t0 = st_t0[s]
    pieces = acc.shape[-1] // LANES
    tile_rows = MOE_TM * pieces

    def hbm_tile(ref, tl):
        return ref.at[pl.ds(pl.multiple_of(tl * tile_rows, tile_rows), tile_rows)]

    def tile_in(q):
        return pltpu.make_async_copy(hbm_tile(xs_hbm, t0 + q), hbm_tile(tbuf, q & 1), sem_g.at[q & 1])

    def tile_out(q, tl):
        return pltpu.make_async_copy(hbm_tile(tbuf, q & 1), hbm_tile(y_hbm, tl), sem_s.at[q & 1])

    def for_tiles(lo, hi, fn):
        def body(q, c):
            fn(q)
            return c
        lax.fori_loop(lo, hi, body, 0)

    @pl.when((s == 0) & (j == 0))
    def _zero_tail():
        tbuf[pl.ds(0, tile_rows), :] = jnp.zeros((tile_rows, LANES), F32)
        for_tiles(n_used[0], n_tiles, lambda tl: tile_out(0, tl).start())
        for_tiles(n_used[0], n_tiles, lambda tl: tile_out(0, tl).wait())

    @pl.when(nt > 0)
    def _active():
        @pl.when(j == 0)
        def _load():
            tile_in(0).start()

            def load(q):
                @pl.when(q + 1 < nt)
                def _():
                    tile_in(q + 1).start()

                tile_in(q).wait()
                _from_slab(xs_b, q * MOE_TM, tbuf, (q & 1) * MOE_TM, MOE_TM // (2 * SUBLANES))
                acc[pl.ds(pl.multiple_of(q * MOE_TM, MOE_TM), MOE_TM), :] = jnp.broadcast_to(
                    bd_ref[...], (MOE_TM, acc.shape[-1]))

            for_tiles(0, nt, load)

        wgb[...] = wg_ref[...].astype(BF16)
        wub[...] = wu_ref[...].astype(BF16)
        wdb[...] = wd_ref[...].astype(BF16)

        def ffn(tile0, n_tiles_chunk):
            rows = n_tiles_chunk * MOE_TM
            r0 = pl.multiple_of(tile0 * MOE_TM, MOE_TM)
            xt = xs_b[pl.ds(r0, rows), :]
            g = jnp.dot(xt, wgb[...], preferred_element_type=F32) + bg_ref[...]
            u = jnp.dot(xt, wub[...], preferred_element_type=F32) + bu_ref[...]
            g = jnp.minimum(g, SWIGLU_LIMIT)
            u = jnp.clip(u, -SWIGLU_LIMIT, SWIGLU_LIMIT)
            hdn = g * jax.nn.sigmoid(SWIGLU_ALPHA * g) * (u + 1.0)
            acc[pl.ds(r0, rows), :] += jnp.dot(hdn.astype(BF16), wdb[...], preferred_element_type=F32)

        big = MOE_CHUNK_TILES
        for_tiles(0, nt // big, lambda c: ffn(c * big, big))
        done = (nt // big) * big
        piece = big // 2
        while piece >= 1:
            @pl.when((nt & piece) != 0)
            def _(done=done, piece=piece):
                ffn(done, piece)

            done = done + (nt & piece)
            piece //= 2

        @pl.when(j == n_j - 1)
        def _store():
            def store(q):
                @pl.when(q >= 2)
                def _():
                    tile_out(q - 2, t0 + q - 2).wait()

                _to_slab(tbuf, (q & 1) * MOE_TM, acc, q * MOE_TM, MOE_TM // SUBLANES)
                tile_out(q, t0 + q).start()

            for_tiles(0, nt, store)

            @pl.when(nt >= 2)
            def _():
                tile_out(nt - 2, t0 + nt - 2).wait()

            tile_out(nt - 1, t0 + nt - 1).wait()


def _experts(xs_slab, tables, num_super_tiles, w_gate, b_gate, w_up, b_up, w_down, b_down):
    st_e, st_t0, st_nt, n_used = tables
    ne, d, f = w_gate.shape
    pieces = d // LANES
    n_tiles = xs_slab.shape[0] // (MOE_TM * pieces)
    n_j = f // MOE_TF
    super_rows = MOE_NTS * MOE_TM

    def w_in_map(s, j, st_e, st_t0, st_nt, n_used):
        return (st_e[s], 0, j)

    def w_down_map(s, j, st_e, st_t0, st_nt, n_used):
        return (st_e[s], j, 0)

    def b_down_map(s, j, st_e, st_t0, st_nt, n_used):
        return (st_e[s], 0, 0)

    any_spec = pl.BlockSpec(memory_space=pl.ANY)
    grid_spec = pltpu.PrefetchScalarGridSpec(
        num_scalar_prefetch=4,
        grid=(num_super_tiles, n_j),
        in_specs=[
            any_spec,
            pl.BlockSpec((None, d, MOE_TF), w_in_map),
            pl.BlockSpec((None, d, MOE_TF), w_in_map),
            pl.BlockSpec((None, MOE_TF, d), w_down_map),
            pl.BlockSpec((None, 1, MOE_TF), w_in_map),
            pl.BlockSpec((None, 1, MOE_TF), w_in_map),
            pl.BlockSpec((None, 1, d), b_down_map),
        ],
        out_specs=any_spec,
        scratch_shapes=[
            pltpu.VMEM((2 * MOE_TM * pieces, LANES), F32),
            pltpu.VMEM((super_rows, d), BF16),
            pltpu.VMEM((super_rows, d), F32),
            pltpu.VMEM((d, MOE_TF), BF16),
            pltpu.VMEM((d, MOE_TF), BF16),
            pltpu.VMEM((MOE_TF, d), BF16),
            pltpu.SemaphoreType.DMA((2,)),
            pltpu.SemaphoreType.DMA((2,)),
        ],
    )
    return pl.pallas_call(
        functools.partial(_experts_kernel, n_j=n_j, n_tiles=n_tiles),
        grid_spec=grid_spec,
        out_shape=jax.ShapeDtypeStruct(xs_slab.shape, F32),
        compiler_params=_cparams(("arbitrary", "arbitrary")),
        name="experts",
    )(st_e, st_t0, st_nt, n_used, xs_slab,
      w_gate, w_up, w_down, b_gate.reshape(ne, 1, f), b_up.reshape(ne, 1, f),
      b_down.reshape(ne, 1, d))


def _combine_kernel(pos_ref, pos_next_ref, h_ref, rf_ref, g_ref, y_hbm, o_ref, yv, acc_ref, sem, *,
                    final_norm):
    i = pl.program_id(0)
    tm, d = h_ref.shape
    pieces = d // LANES
    slot_rows = TOP_K * tm
    slot = i & 1

    def row_copy(src, dst, sl):
        return pltpu.make_async_copy(
            y_hbm.at[pl.ds(pl.multiple_of(src * pieces, pieces), pieces)],
            yv.at[pl.ds(pl.multiple_of((sl * slot_rows + dst) * pieces, pieces), pieces)], sem.at[sl])

    def gather(table, sl):
        def token(tok, c):
            for k in range(TOP_K):
                row_copy(table[tok * TOP_K + k], k * tm + tok, sl).start(priority=k % 2)
            return c

        lax.fori_loop(0, tm, token, 0, unroll=4)

    @pl.when(i == 0)
    def _():
        gather(pos_ref, 0)

    @pl.when(i + 1 < pl.num_programs(0))
    def _():
        gather(pos_next_ref, 1 - slot)

    def drain(m, c):
        for _ in range(LANES):
            row_copy(0, 0, slot).wait()
        return c

    lax.fori_loop(0, slot_rows // LANES, drain, 0)

    def group(g, c):
        r0 = pl.multiple_of(g * SUBLANES, SUBLANES)
        rf = rf_ref[pl.ds(r0, SUBLANES), :]
        gates = [jnp.broadcast_to(rf[:, k:k + 1], (SUBLANES, LANES)) for k in range(TOP_K)]
        for p in range(pieces):
            v = h_ref[pl.ds(r0, SUBLANES), p * LANES:(p + 1) * LANES]
            for k in range(TOP_K):
                row = (slot * slot_rows + k * tm + r0) * pieces + p
                v = v + gates[k] * yv[pl.ds(row, SUBLANES, stride=pieces), :]
            acc_ref[pl.ds(r0, SUBLANES), p * LANES:(p + 1) * LANES] = v
        return c

    lax.fori_loop(0, tm // SUBLANES, group, 0)
    o_ref[...] = _rms(acc_ref[...], g_ref[...]) if final_norm else acc_ref[...]


def _combine(h, y_slab, pos, route_f, g, final_norm):
    t, d = h.shape
    pieces = d // LANES
    tm = min(COMBINE_TM, t)
    n_steps = t // tm
    table = lambda index: pl.BlockSpec((tm * TOP_K,), index, memory_space=pltpu.SMEM)
    return pl.pallas_call(
        functools.partial(_combine_kernel, final_norm=final_norm),
        grid=(n_steps,),
        in_specs=[table(lambda i: (i,)),
                  table(lambda i: (jnp.minimum(i + 1, n_steps - 1),)),
                  pl.BlockSpec((tm, d), lambda i: (i, 0)),
                  pl.BlockSpec((tm, LANES), lambda i: (i, 0)),
                  pl.BlockSpec((1, d), lambda i: (0, 0)),
                  pl.BlockSpec(memory_space=pl.ANY)],
        out_specs=pl.BlockSpec((tm, d), lambda i: (i, 0)),
        out_shape=jax.ShapeDtypeStruct((t, d), F32),
        scratch_shapes=[pltpu.VMEM((2 * TOP_K * tm * pieces, LANES), F32), pltpu.VMEM((tm, d), F32),
                        pltpu.SemaphoreType.DMA((2,))],
        compiler_params=_cparams(("arbitrary",)),
        name="combine",
    )(pos.reshape(-1), pos.reshape(-1), h, route_f, g.reshape(1, d), y_slab)


def kernel(x, norm1_g, w_in, rel_bias, ssm_lambda_re, ssm_lambda_im, ssm_log_step, ssm_b_re, ssm_b_im, ssm_c_re, ssm_c_im, ssm_d, w_ssm_glu, b_ssm_glu, attn_out_g, ssm_out_g, w_out, norm2_g, w_router, b_router, w_gate, b_gate, w_up, b_up, w_down, b_down, norm_f_g):
    b, l, d = x.shape
    depth = w_in.shape[0]
    aw = attn_out_g.shape[-1]
    n_heads = aw // HEAD_DIM
    n_experts = w_router.shape[-1]
    t = b * l
    assert l % ATT_TQ == 0 and l >= ATT_WIN and l % S5_TC == 0
    assert (t * TOP_K) % MOE_TM == 0 and t % DISPATCH_TM == 0

    h = x.reshape(t, d).astype(F32)
    for li in range(depth):
        last = li == depth - 1
        proj = _inproj(h, norm1_g[li], w_in[li].astype(BF16), tm=min(1024, t), tn=512)
        proj3 = proj.reshape(b, l, -1)
        mixed_a = _attention(proj3, _attn_bias_table(rel_bias[li]), attn_out_g[li], n_heads)
        wbd, cbd, avec = _s5_params(ssm_lambda_re[li], ssm_lambda_im[li], ssm_log_step[li],
                                    ssm_b_re[li], ssm_b_im[li], ssm_c_re[li], ssm_c_im[li])
        mixed_s = _s5(proj3, wbd, cbd, avec, ssm_d[li], w_ssm_glu[li].astype(BF16), b_ssm_glu[li],
                      ssm_out_g[li], u_block_index=3 * aw // (d - aw))
        h, n2_slab, route_i, route_f, cnt = _outproj(
            mixed_a.reshape(t, aw), mixed_s.reshape(t, d - aw), h, w_out[li].astype(BF16),
            norm2_g[li], w_router[li], b_router[li], tm=min(512, t))
        pos, st_e, st_t0, st_nt, pad_tile, n_used, num_st = _route_tables(
            route_i, cnt[0, :n_experts].astype(I32), t, n_experts)
        xs_slab = _dispatch(n2_slab, pos, pad_tile, d)
        ys_slab = _experts(xs_slab, (st_e, st_t0, st_nt, n_used), num_st, w_gate[li], b_gate[li],
                           w_up[li], b_up[li], w_down[li], b_down[li])
        g_fin = norm_f_g if last else jnp.ones((d,), F32)
        h = _combine(h, ys_slab, pos, route_f, g_fin, final_norm=last)
    return h.reshape(b, l, d).astype(x.dtype)
```

```python
import functools
import math

import jax
import jax.numpy as jnp
from jax import lax
from jax.experimental import pallas as pl
from jax.experimental.pallas import tpu as pltpu

F32 = jnp.float32
BF16 = jnp.bfloat16
I32 = jnp.int32

CHUNK = 64
HEAD_DIM = 128
N_BACK_CHUNKS = 8
REL_CLIP = 128
SSM_GROUP = 16
SSM_STATE = 64
TOP_K = 4
SWIGLU_LIMIT = 7.0
SWIGLU_ALPHA = 1.702
EPS = 1e-5
NEG_BIG = -1e30

LANES = 128
SUBLANES = 8
MXU_DIM = 256
VMEM_LIMIT_BYTES = 56 * 1024 * 1024

ATT_TQ = 2 * CHUNK
ATT_WIN = (N_BACK_CHUNKS + 2) * CHUNK
ATT_NSHIFT = (N_BACK_CHUNKS * CHUNK) // ATT_TQ + 1

S5_GROUPS_PER_BLOCK = MXU_DIM // SSM_GROUP
S5_BLOCK_STATES = S5_GROUPS_PER_BLOCK * SSM_STATE
S5_STRIP = 512
S5_TC = 32

MOE_TM = 256
MOE_NTS = 9
MOE_TF = 256
MOE_CHUNK_TILES = 4
COMBINE_PITCH = 24
DISPATCH_TM = 512
COMBINE_TM = 256


def _cparams(sem):
    return pltpu.CompilerParams(dimension_semantics=sem, vmem_limit_bytes=VMEM_LIMIT_BYTES)


def _rms(x, g):
    r = lax.rsqrt(jnp.mean(x * x, axis=-1, keepdims=True) + EPS)
    return x * r * g


def _to_slab(slab_ref, slab_row0, src_ref, src_row0, n_groups, pitch=None):
    pieces = src_ref.shape[-1] // LANES
    pitch = pitch or pieces

    def group(g, c):
        r0 = g * SUBLANES
        src0 = pl.multiple_of(src_row0 + r0, SUBLANES)
        for p in range(pieces):
            slab_ref[pl.ds((slab_row0 + r0) * pitch + p, SUBLANES, stride=pitch), :] = (
                src_ref[pl.ds(src0, SUBLANES), p * LANES:(p + 1) * LANES])
        return c

    lax.fori_loop(0, n_groups, group, 0)


def _from_slab(dst_ref, dst_row0, slab_ref, slab_row0, n_groups, pitch=None):
    pieces = dst_ref.shape[-1] // LANES
    pitch = pitch or pieces
    rows = 2 * SUBLANES

    def group(g, c):
        r0 = g * rows
        dst0 = pl.multiple_of(dst_row0 + r0, rows)
        for p in range(pieces):
            lo = slab_ref[pl.ds((slab_row0 + r0) * pitch + p, SUBLANES, stride=pitch), :]
            hi = slab_ref[pl.ds((slab_row0 + r0 + SUBLANES) * pitch + p, SUBLANES, stride=pitch), :]
            dst_ref[pl.ds(dst0, rows), p * LANES:(p + 1) * LANES] = (
                jnp.concatenate([lo, hi], axis=0).astype(dst_ref.dtype))
        return c

    lax.fori_loop(0, n_groups, group, 0)


def _inproj_kernel(x_ref, g_ref, w_ref, o_ref, nb_ref):
    @pl.when(pl.program_id(1) == 0)
    def _():
        nb_ref[...] = _rms(x_ref[...], g_ref[...]).astype(BF16)

    o_ref[...] = jnp.dot(nb_ref[...], w_ref[...], preferred_element_type=F32).astype(o_ref.dtype)


def _inproj(x2d, g, w_bf16, tm, tn):
    t, d = x2d.shape
    n = w_bf16.shape[1]
    return pl.pallas_call(
        _inproj_kernel,
        grid=(t // tm, n // tn),
        in_specs=[
            pl.BlockSpec((tm, d), lambda i, j: (i, 0)),
            pl.BlockSpec((1, d), lambda i, j: (0, 0)),
            pl.BlockSpec((d, tn), lambda i, j: (0, j)),
        ],
        out_specs=pl.BlockSpec((tm, tn), lambda i, j: (i, j)),
        out_shape=jax.ShapeDtypeStruct((t, n), BF16),
        scratch_shapes=[pltpu.VMEM((tm, d), BF16)],
        compiler_params=_cparams(("arbitrary", "arbitrary")),
        name="inproj",
    )(x2d, g.reshape(1, d), w_bf16)


def _attn_bias_table(rel_bias):
    n_heads = rel_bias.shape[0]
    period = ATT_WIN + ATT_TQ
    m = jnp.arange(period)
    diff = jnp.where(m < ATT_WIN, m, m - period)
    sh = jnp.arange(ATT_NSHIFT)[:, None]
    rel = N_BACK_CHUNKS * CHUNK - ATT_TQ * sh - diff[None, :]
    idx = jnp.clip(rel, -REL_CLIP, REL_CLIP) + REL_CLIP
    vext = rel_bias.astype(F32)[:, idx]
    flat = jnp.tile(vext, (1, 1, ATT_TQ))[..., :ATT_TQ * (period - 1)]
    tbl = flat.reshape(n_heads, ATT_NSHIFT, ATT_TQ, period - 1)[..., :ATT_WIN]
    return jnp.transpose(tbl, (1, 0, 2, 3))


def _attn_kernel(q_ref, k_ref, v_ref, b_ref, g_ref, o_ref, a_ref, *, n_heads):
    i = pl.program_id(1)
    s0 = pl.multiple_of(jnp.maximum(i * ATT_TQ - N_BACK_CHUNKS * CHUNK, 0), ATT_TQ)
    row = lax.broadcasted_iota(I32, (ATT_TQ, ATT_WIN), 0)
    col = lax.broadcasted_iota(I32, (ATT_TQ, ATT_WIN), 1)
    qc = (i * ATT_TQ + row) // CHUNK
    kc = (s0 + col) // CHUNK
    ok = (kc <= qc) & (kc >= qc - N_BACK_CHUNKS)
    scale = 1.0 / math.sqrt(HEAD_DIM)
    for h in range(n_heads):
        hs = slice(h * HEAD_DIM, (h + 1) * HEAD_DIM)
        qh = q_ref[:, hs]
        kh = k_ref[pl.ds(s0, ATT_WIN), hs]
        vh = v_ref[pl.ds(s0, ATT_WIN), hs]
        s = lax.dot_general(qh, kh, (((1,), (1,)), ((), ())), preferred_element_type=F32)
        s = jnp.where(ok, s * scale + b_ref[h], NEG_BIG)
        m = jnp.max(s, axis=-1, keepdims=True)
        p = jnp.exp(s - m)
        l = jnp.sum(p, axis=-1, keepdims=True)
        o = jnp.dot(p.astype(BF16), vh, preferred_element_type=F32)
        a_ref[:, hs] = o / l
    o_ref[...] = _rms(a_ref[...], g_ref[...]).astype(o_ref.dtype)


def _attention(proj3, bias_tbl, g, n_heads):
    b, l, _ = proj3.shape
    aw = n_heads * HEAD_DIM
    nsh = ATT_NSHIFT
    return pl.pallas_call(
        functools.partial(_attn_kernel, n_heads=n_heads),
        grid=(b, l // ATT_TQ),
        in_specs=[
            pl.BlockSpec((None, ATT_TQ, aw), lambda bi, i: (bi, i, 0)),
            pl.BlockSpec((None, l, aw), lambda bi, i: (bi, 0, 1)),
            pl.BlockSpec((None, l, aw), lambda bi, i: (bi, 0, 2)),
            pl.BlockSpec((None, n_heads, ATT_TQ, ATT_WIN),
                         lambda bi, i: (jnp.maximum(nsh - 1 - i, 0), 0, 0, 0)),
            pl.BlockSpec((1, aw), lambda bi, i: (0, 0)),
        ],
        out_specs=pl.BlockSpec((None, ATT_TQ, aw), lambda bi, i: (bi, i, 0)),
        out_shape=jax.ShapeDtypeStruct((b, l, aw), BF16),
        scratch_shapes=[pltpu.VMEM((ATT_TQ, aw), F32)],
        compiler_params=_cparams(("arbitrary", "arbitrary")),
        name="attn",
    )(proj3, proj3, proj3, bias_tbl, g.reshape(1, aw))


def _s5_params(lam_re, lam_im, log_step, b_re, b_im, c_re, c_im):
    g, p = lam_re.shape
    nb = g // S5_GROUPS_PER_BLOCK
    gl = S5_GROUPS_PER_BLOCK
    dt = jnp.exp(log_step.astype(F32))[:, None]
    lr = lam_re.astype(F32)
    li = lam_im.astype(F32)
    mag = jnp.exp(lr * dt)
    ab_re = mag * jnp.cos(li * dt)
    ab_im = mag * jnp.sin(li * dt)
    den = lr * lr + li * li
    nr = ab_re - 1.0
    ni = ab_im
    f_re = (nr * lr + ni * li) / den
    f_im = (ni * lr - nr * li) / den
    br = b_re.astype(F32)
    bi = b_im.astype(F32)
    bb_re = f_re[..., None] * br - f_im[..., None] * bi
    bb_im = f_re[..., None] * bi + f_im[..., None] * br
    eye = jnp.eye(gl, dtype=F32)
    bb = jnp.stack([bb_re, bb_im]).reshape(2, nb, gl, p, SSM_GROUP)
    wbd = jnp.einsum("ajgpc,gh->jgcahp", bb, eye).reshape(nb, gl * SSM_GROUP, 2 * gl * p)
    cc = jnp.stack([c_re.astype(F32), -c_im.astype(F32)]).reshape(2, nb, gl, SSM_GROUP, p)
    cbd = jnp.einsum("ajgcp,gh->jagphc", cc, eye).reshape(nb, 2 * gl * p, gl * SSM_GROUP)
    avec = jnp.stack([ab_re, ab_im]).reshape(2, nb, gl, p)
    avec = jnp.transpose(avec, (1, 0, 2, 3)).reshape(1, 2 * g * p)
    return wbd.astype(BF16), cbd.astype(BF16), avec


def _s5_kernel(u_ref, pm_ref, pmt_ref, wbd_ref, cbd_ref, av_ref, d_ref, wglu_ref, bglu_ref,
               g_ref, o_ref, xs_ref, hst_ref, *, n_blocks, tc, batch):
    rows = tc * batch
    sw = u_ref.shape[-1]
    bw = 2 * S5_BLOCK_STATES
    cw = S5_GROUPS_PER_BLOCK * SSM_GROUP

    @pl.when(pl.program_id(0) == 0)
    def _():
        hst_ref[...] = jnp.zeros_like(hst_ref)

    u_tm = jnp.dot(pm_ref[...], u_ref[...].reshape(rows, sw), preferred_element_type=F32)
    u_tmb = u_tm.astype(BF16)
    for j in range(n_blocks):
        xs_ref[:, j * bw:(j + 1) * bw] = jnp.dot(
            u_tmb[:, j * cw:(j + 1) * cw], wbd_ref[j], preferred_element_type=F32)

    for j in range(n_blocks):
        for s in range(S5_BLOCK_STATES // S5_STRIP):
            cr = j * bw + s * S5_STRIP
            ci = cr + S5_BLOCK_STATES
            a_re = jnp.broadcast_to(av_ref[:, cr:cr + S5_STRIP], (batch, S5_STRIP))
            a_im = jnp.broadcast_to(av_ref[:, ci:ci + S5_STRIP], (batch, S5_STRIP))

            def step(t, carry, cr=cr, ci=ci, a_re=a_re, a_im=a_im):
                h_re, h_im = carry
                r0 = pl.multiple_of(t * batch, batch)
                n_re = a_re * h_re - a_im * h_im + xs_ref[pl.ds(r0, batch), cr:cr + S5_STRIP]
                n_im = a_re * h_im + a_im * h_re + xs_ref[pl.ds(r0, batch), ci:ci + S5_STRIP]
                xs_ref[pl.ds(r0, batch), cr:cr + S5_STRIP] = n_re
                xs_ref[pl.ds(r0, batch), ci:ci + S5_STRIP] = n_im
                return n_re, n_im

            h_re, h_im = lax.fori_loop(
                0, tc, step, (hst_ref[:, cr:cr + S5_STRIP], hst_ref[:, ci:ci + S5_STRIP]), unroll=4)
            hst_ref[:, cr:cr + S5_STRIP] = h_re
            hst_ref[:, ci:ci + S5_STRIP] = h_im

    ys = [jnp.dot(xs_ref[:, j * bw:(j + 1) * bw].astype(BF16), cbd_ref[j], preferred_element_type=F32)
          for j in range(n_blocks)]
    y = jnp.concatenate(ys, axis=1) + d_ref[...] * u_tm
    y = jax.nn.gelu(y)
    z = jnp.dot(y.astype(BF16), wglu_ref[...], preferred_element_type=F32) + bglu_ref[...]
    y = y * jax.nn.sigmoid(z)
    res = _rms(y, g_ref[...]).astype(BF16)
    out_bm = jnp.dot(pmt_ref[...], res, preferred_element_type=F32).astype(o_ref.dtype)
    o_ref[...] = out_bm.reshape(batch, tc, sw)


def _s5(proj3, wbd, cbd, avec, d, wglu_bf16, bglu, g, u_block_index):
    b, l, _ = proj3.shape
    assert b == SUBLANES, "the scan keeps the batch on the sublane axis"
    n_blocks, cw, bw = wbd.shape
    sw = n_blocks * cw
    tc = S5_TC
    rows = tc * b
    r = jnp.arange(rows)
    src = (r % b) * tc + r // b
    pm = (src[:, None] == jnp.arange(rows)[None, :]).astype(BF16)
    pmt = pm.T
    const = lambda shape: pl.BlockSpec(shape, lambda c: (0,) * len(shape))
    return pl.pallas_call(
        functools.partial(_s5_kernel, n_blocks=n_blocks, tc=tc, batch=b),
        grid=(l // tc,),
        in_specs=[
            pl.BlockSpec((b, tc, sw), lambda c: (0, c, u_block_index)),
            const((rows, rows)), const((rows, rows)),
            const((n_blocks, cw, bw)), const((n_blocks, bw, cw)),
            const((1, n_blocks * bw)), const((1, sw)),
            const((sw, sw)), const((1, sw)), const((1, sw)),
        ],
        out_specs=pl.BlockSpec((b, tc, sw), lambda c: (0, c, 0)),
        out_shape=jax.ShapeDtypeStruct((b, l, sw), BF16),
        scratch_shapes=[pltpu.VMEM((rows, n_blocks * bw), F32),
                        pltpu.VMEM((b, n_blocks * bw), F32)],
        compiler_params=_cparams(("arbitrary",)),
        name="s5",
    )(proj3, pm, pmt, wbd, cbd, avec, d.reshape(1, sw), wglu_bf16, bglu.reshape(1, sw),
      g.reshape(1, sw))


def _outproj_kernel(ma_ref, ms_ref, x_ref, wo_ref, g2_ref, wr_ref, br_ref, tri_ref,
                    h_ref, n2s_ref, ri_ref, rf_ref, cnt_ref, carry_ref, n2_ref):
    aw = ma_ref.shape[-1]

    @pl.when(pl.program_id(0) == 0)
    def _():
        carry_ref[...] = jnp.zeros_like(carry_ref)

    h = (x_ref[...]
         + jnp.dot(ma_ref[...], wo_ref[:aw, :], preferred_element_type=F32)
         + jnp.dot(ms_ref[...], wo_ref[aw:, :], preferred_element_type=F32))
    h_ref[...] = h
    n2 = _rms(h, g2_ref[...])
    n2_ref[...] = n2
    _to_slab(n2s_ref, 0, n2_ref, 0, n2.shape[0] // SUBLANES)

    n_hi = n2.astype(BF16)
    n_lo = (n2 - n_hi.astype(F32)).astype(BF16)
    w = wr_ref[...]
    w_hi = w.astype(BF16)
    w_lo = (w - w_hi.astype(F32)).astype(BF16)
    logits = (jnp.dot(n_hi, w_hi, preferred_element_type=F32)
              + jnp.dot(n_lo, w_hi, preferred_element_type=F32)
              + jnp.dot(n_hi, w_lo, preferred_element_type=F32)) + br_ref[...]

    tm, ne = logits.shape
    lane = lax.broadcasted_iota(I32, (tm, ne), 1)
    work = logits
    vals, idxs = [], []
    for _ in range(TOP_K):
        m = jnp.max(work, axis=-1, keepdims=True)
        ik = jnp.min(jnp.where(work == m, lane, ne), axis=-1, keepdims=True)
        vals.append(m)
        idxs.append(ik)
        work = jnp.where(lane == ik, -jnp.inf, work)
    es = [jnp.exp(v - vals[0]) for v in vals]
    den = es[0] + es[1] + es[2] + es[3]

    hot = jnp.zeros((tm, ne), F32)
    for ik in idxs:
        hot = hot + (lane == ik).astype(F32)
    before = jnp.dot(tri_ref[...], hot.astype(BF16), preferred_element_type=F32) + carry_ref[...]
    ri = jnp.zeros((tm, ne), I32)
    rf = jnp.zeros((tm, ne), F32)
    for k in range(TOP_K):
        rank = jnp.sum(jnp.where(lane == idxs[k], before, 0.0), axis=-1, keepdims=True)
        ri = jnp.where(lane == k, idxs[k], ri)
        ri = jnp.where(lane == TOP_K + k, rank.astype(I32), ri)
        rf = jnp.where(lane == k, es[k] / den, rf)
    ri_ref[...] = ri
    rf_ref[...] = rf
    carry_ref[...] = carry_ref[...] + jnp.sum(hot, axis=0, keepdims=True)
    cnt_ref[...] = carry_ref[...]


def _outproj(ma, ms, x2d, wo_bf16, g2, w_router, b_router, tm):
    t, d = x2d.shape
    aw = ma.shape[1]
    ne = w_router.shape[1]
    pieces = d // LANES
    wr = jnp.zeros((d, LANES), F32).at[:, :ne].set(w_router.astype(F32))
    br = jnp.full((1, LANES), NEG_BIG, F32).at[0, :ne].set(b_router.astype(F32))
    tri = (jnp.arange(tm)[:, None] > jnp.arange(tm)[None, :]).astype(BF16)
    const = lambda shape: pl.BlockSpec(shape, lambda i: (0,) * len(shape))
    row = lambda w: pl.BlockSpec((tm, w), lambda i: (i, 0))
    return pl.pallas_call(
        _outproj_kernel,
        grid=(t // tm,),
        in_specs=[row(aw), row(ms.shape[1]), row(d), const((d, d)), const((1, d)),
                  const((d, LANES)), const((1, LANES)), const((tm, tm))],
        out_specs=[row(d), pl.BlockSpec((tm * pieces, LANES), lambda i: (i, 0)),
                   row(LANES), row(LANES), const((1, LANES))],
        out_shape=[jax.ShapeDtypeStruct((t, d), F32), jax.ShapeDtypeStruct((t * pieces, LANES), F32),
                   jax.ShapeDtypeStruct((t, LANES), I32), jax.ShapeDtypeStruct((t, LANES), F32),
                   jax.ShapeDtypeStruct((1, LANES), F32)],
        scratch_shapes=[pltpu.VMEM((1, LANES), F32), pltpu.VMEM((tm, d), F32)],
        compiler_params=_cparams(("arbitrary",)),
        name="outproj",
    )(ma, ms, x2d, wo_bf16, g2.reshape(1, d), wr, br, tri)


def _route_tables(route_i, counts, n_tokens, n_experts):
    idx = route_i[:, :TOP_K]
    rank = route_i[:, TOP_K:2 * TOP_K]
    n_tiles_max = n_tokens * TOP_K // MOE_TM + n_experts
    s_max = n_experts + n_tiles_max // MOE_NTS
    e_iota = jnp.arange(n_experts, dtype=I32)

    tiles_e = (counts + MOE_TM - 1) // MOE_TM
    tile_end = jnp.cumsum(tiles_e)
    tile_off = tile_end - tiles_e
    off_of = jnp.sum(jnp.where(idx[..., None] == e_iota, tile_off, 0), axis=-1)
    pos = (off_of * MOE_TM + rank).astype(I32)

    nst_e = (tiles_e + MOE_NTS - 1) // MOE_NTS
    st_incl = jnp.cumsum(nst_e)
    num_st = st_incl[-1]
    s = jnp.arange(s_max, dtype=I32)
    active = s < num_st
    s_eff = jnp.minimum(s, num_st - 1)
    e_s = jnp.minimum(jnp.sum(st_incl[None, :] <= s_eff[:, None], axis=1), n_experts - 1).astype(I32)
    onehot = e_s[:, None] == e_iota[None, :]
    pick = lambda v: jnp.sum(jnp.where(onehot, v[None, :], 0), axis=1)
    local = s_eff - (pick(st_incl) - pick(nst_e))
    tile0 = (pick(tile_off) + MOE_NTS * local).astype(I32)
    nt = jnp.where(active, jnp.minimum(MOE_NTS, pick(tiles_e) - MOE_NTS * local), 0).astype(I32)

    tile_ids = jnp.arange(n_tiles_max, dtype=I32)
    is_last = jnp.any((tile_ids[:, None] == tile_end[None, :] - 1) & (tiles_e[None, :] > 0), axis=1)
    pad_tile = (is_last | (tile_ids >= tile_end[-1])).astype(I32)
    return pos, e_s, tile0, nt, pad_tile, tile_end[-1:].astype(I32), num_st.astype(I32)


def _dispatch_kernel(pad_tile, pos_ref, n2_ref, xs_hbm, zero_ref, sem_z, sem_r, *,
                     tokens_per_step, pieces):
    i = pl.program_id(0)
    tile_rows = MOE_TM * pieces
    n_tiles = pad_tile.shape[0]

    @pl.when(i == 0)
    def _zero_fill():
        zero_ref[...] = jnp.zeros_like(zero_ref)

        def fill(tl):
            start = pl.multiple_of(tl * tile_rows, tile_rows)
            return pltpu.make_async_copy(zero_ref, xs_hbm.at[pl.ds(start, tile_rows)], sem_z)

        def issue(tl, c):
            @pl.when(pad_tile[tl] > 0)
            def _():
                fill(tl).start()
            return c

        def drain(tl, c):
            @pl.when(pad_tile[tl] > 0)
            def _():
                fill(tl).wait()
            return c

        lax.fori_loop(0, n_tiles, issue, 0)
        lax.fori_loop(0, n_tiles, drain, 0)

    def row_copy(tok, dst):
        return pltpu.make_async_copy(
            n2_ref.at[pl.ds(pl.multiple_of(tok * pieces, pieces), pieces)],
            xs_hbm.at[pl.ds(pl.multiple_of(dst * pieces, pieces), pieces)], sem_r)

    def token(tok, c):
        for k in range(TOP_K):
            row_copy(tok, pos_ref[tok * TOP_K + k]).start(priority=k % 2)
        return c

    lax.fori_loop(0, tokens_per_step, token, 0, unroll=4)

    def drain(m, c):
        for _ in range(LANES):
            row_copy(0, 0).wait()
        return c

    lax.fori_loop(0, tokens_per_step * TOP_K // LANES, drain, 0)


def _dispatch(n2_slab, pos, pad_tile, d):
    pieces = d // LANES
    n_tokens = n2_slab.shape[0] // pieces
    n_tiles = pad_tile.shape[0]
    tm = min(DISPATCH_TM, n_tokens)
    grid_spec = pltpu.PrefetchScalarGridSpec(
        num_scalar_prefetch=1,
        grid=(n_tokens // tm,),
        in_specs=[pl.BlockSpec((tm * TOP_K,), lambda i, pad: (i,), memory_space=pltpu.SMEM),
                  pl.BlockSpec((tm * pieces, LANES), lambda i, pad: (i, 0))],
        out_specs=pl.BlockSpec(memory_space=pl.ANY),
        scratch_shapes=[pltpu.VMEM((MOE_TM * pieces, LANES), F32),
                        pltpu.SemaphoreType.DMA(()), pltpu.SemaphoreType.DMA(())],
    )
    return pl.pallas_call(
        functools.partial(_dispatch_kernel, tokens_per_step=tm, pieces=pieces),
        grid_spec=grid_spec,
        out_shape=jax.ShapeDtypeStruct((n_tiles * MOE_TM * pieces, LANES), F32),
        compiler_params=_cparams(("arbitrary",)),
        name="dispatch",
    )(pad_tile, pos.reshape(-1), n2_slab)


def _experts_kernel(st_e, st_t0, st_nt, n_used, xs_hbm, wg_ref, wu_ref, wd_ref, bg_ref, bu_ref, bd_ref,
                    y_hbm, tbuf, xs_b, acc, wgb, wub, wdb, sem_g, sem_s, *, n_j, n_tiles):
    s = pl.program_id(0)
    j = pl.program_id(1)
    nt = st_nt[s]
    t0 = st_t0[s]
    pieces = acc.shape[-1] // LANES
    tile_rows = MOE_TM * pieces

    def hbm_tile(ref, tl):
        return ref.at[pl.ds(pl.multiple_of(tl * tile_rows, tile_rows), tile_rows)]

    def tile_in(q):
        return pltpu.make_async_copy(hbm_tile(xs_hbm, t0 + q), hbm_tile(tbuf, q & 1), sem_g.at[q & 1])

    def tile_out(q, tl):
        return pltpu.make_async_copy(hbm_tile(tbuf, q & 1), hbm_tile(y_hbm, tl), sem_s.at[q & 1])

    def for_tiles(lo, hi, fn):
        def body(q, c):
            fn(q)
            return c
        lax.fori_loop(lo, hi, body, 0)

    @pl.when((s == 0) & (j == 0))
    def _zero_tail():
        tbuf[pl.ds(0, tile_rows), :] = jnp.zeros((tile_rows, LANES), F32)
        for_tiles(n_used[0], n_tiles, lambda tl: tile_out(0, tl).start())
        for_tiles(n_used[0], n_tiles, lambda tl: tile_out(0, tl).wait())

    @pl.when(nt > 0)
    def _active():
        @pl.when(j == 0)
        def _load():
            tile_in(0).start()

            def load(q):
                @pl.when(q + 1 < nt)
                def _():
                    tile_in(q + 1).start()

                tile_in(q).wait()
                _from_slab(xs_b, q * MOE_TM, tbuf, (q & 1) * MOE_TM, MOE_TM // (2 * SUBLANES))
                acc[pl.ds(pl.multiple_of(q * MOE_TM, MOE_TM), MOE_TM), :] = jnp.broadcast_to(
                    bd_ref[...], (MOE_TM, acc.shape[-1]))

            for_tiles(0, nt, load)

        wgb[...] = wg_ref[...].astype(BF16)
        wub[...] = wu_ref[...].astype(BF16)
        wdb[...] = wd_ref[...].astype(BF16)

        def ffn(tile0, n_tiles_chunk):
            rows = n_tiles_chunk * MOE_TM
            r0 = pl.multiple_of(tile0 * MOE_TM, MOE_TM)
            xt = xs_b[pl.ds(r0, rows), :]
            g = jnp.dot(xt, wgb[...], preferred_element_type=F32) + bg_ref[...]
            u = jnp.dot(xt, wub[...], preferred_element_type=F32) + bu_ref[...]
            g = jnp.minimum(g, SWIGLU_LIMIT)
            u = jnp.clip(u, -SWIGLU_LIMIT, SWIGLU_LIMIT)
            hdn = g * jax.nn.sigmoid(SWIGLU_ALPHA * g) * (u + 1.0)
            acc[pl.ds(r0, rows), :] += jnp.dot(hdn.astype(BF16), wdb[...], preferred_element_type=F32)

        big = MOE_CHUNK_TILES
        for_tiles(0, nt // big, lambda c: ffn(c * big, big))
        done = (nt // big) * big
        piece = big // 2
        while piece >= 1:
            @pl.when((nt & piece) != 0)
            def _(done=done, piece=piece):
                ffn(done, piece)

            done = done + (nt & piece)
            piece //= 2

        @pl.when(j == n_j - 1)
        def _store():
            def store(q):
                @pl.when(q >= 2)
                def _():
                    tile_out(q - 2, t0 + q - 2).wait()

                _to_slab(tbuf, (q & 1) * MOE_TM, acc, q * MOE_TM, MOE_TM // SUBLANES)
                tile_out(q, t0 + q).start()

            for_tiles(0, nt, store)

            @pl.when(nt >= 2)
            def _():
                tile_out(nt - 2, t0 + nt - 2).wait()

            tile_out(nt - 1, t0 + nt - 1).wait()


def _experts(xs_slab, tables, num_super_tiles, w_gate, b_gate, w_up, b_up, w_down, b_down):
    st_e, st_t0, st_nt, n_used = tables
    ne, d, f = w_gate.shape
    pieces = d // LANES
    n_tiles = xs_slab.shape[0] // (MOE_TM * pieces)
    n_j = f // MOE_TF
    super_rows = MOE_NTS * MOE_TM

    def w_in_map(s, j, st_e, st_t0, st_nt, n_used):
        return (st_e[s], 0, j)

    def w_down_map(s, j, st_e, st_t0, st_nt, n_used):
        return (st_e[s], j, 0)

    def b_down_map(s, j, st_e, st_t0, st_nt, n_used):
        return (st_e[s], 0, 0)

    any_spec = pl.BlockSpec(memory_space=pl.ANY)
    grid_spec = pltpu.PrefetchScalarGridSpec(
        num_scalar_prefetch=4,
        grid=(num_super_tiles, n_j),
        in_specs=[
            any_spec,
            pl.BlockSpec((None, d, MOE_TF), w_in_map),
            pl.BlockSpec((None, d, MOE_TF), w_in_map),
            pl.BlockSpec((None, MOE_TF, d), w_down_map),
            pl.BlockSpec((None, 1, MOE_TF), w_in_map),
            pl.BlockSpec((None, 1, MOE_TF), w_in_map),
            pl.BlockSpec((None, 1, d), b_down_map),
        ],
        out_specs=any_spec,
        scratch_shapes=[
            pltpu.VMEM((2 * MOE_TM * pieces, LANES), F32),
            pltpu.VMEM((super_rows, d), BF16),
            pltpu.VMEM((super_rows, d), F32),
            pltpu.VMEM((d, MOE_TF), BF16),
            pltpu.VMEM((d, MOE_TF), BF16),
            pltpu.VMEM((MOE_TF, d), BF16),
            pltpu.SemaphoreType.DMA((2,)),
            pltpu.SemaphoreType.DMA((2,)),
        ],
    )
    return pl.pallas_call(
        functools.partial(_experts_kernel, n_j=n_j, n_tiles=n_tiles),
        grid_spec=grid_spec,
        out_shape=jax.ShapeDtypeStruct(xs_slab.shape, F32),
        compiler_params=_cparams(("arbitrary", "arbitrary")),
        name="experts",
    )(st_e, st_t0, st_nt, n_used, xs_slab,
      w_gate, w_up, w_down, b_gate.reshape(ne, 1, f), b_up.reshape(ne, 1, f),
      b_down.reshape(ne, 1, d))


def _combine_kernel(pos_ref, pos_next_ref, h_ref, rf_ref, g_ref, y_hbm, o_ref, yv, acc_ref, sem, *,
                    final_norm):
    i = pl.program_id(0)
    tm, d = h_ref.shape
    pieces = d // LANES
    slot_rows = TOP_K * tm
    slot = i & 1

    def row_copy(src, dst, sl):
        return pltpu.make_async_copy(
            y_hbm.at[pl.ds(pl.multiple_of(src * pieces, pieces), pieces)],
            yv.at[pl.ds(pl.multiple_of((sl * slot_rows + dst) * COMBINE_PITCH, SUBLANES), pieces)],
            sem.at[sl])

    def gather(table, sl):
        def token(tok, c):
            for k in range(TOP_K):
                row_copy(table[tok * TOP_K + k], k * tm + tok, sl).start(priority=k % 2)
            return c

        lax.fori_loop(0, tm, token, 0, unroll=4)

    @pl.when(i == 0)
    def _():
        gather(pos_ref, 0)

    @pl.when(i + 1 < pl.num_programs(0))
    def _():
        gather(pos_next_ref, 1 - slot)

    def drain(m, c):
        for _ in range(LANES):
            row_copy(0, 0, slot).wait()
        return c

    lax.fori_loop(0, slot_rows // LANES, drain, 0)

    def group(g, c):
        r0 = pl.multiple_of(g * SUBLANES, SUBLANES)
        rf = rf_ref[pl.ds(r0, SUBLANES), :]
        gates = [jnp.broadcast_to(rf[:, k:k + 1], (SUBLANES, LANES)) for k in range(TOP_K)]
        for p in range(pieces):
            v = h_ref[pl.ds(r0, SUBLANES), p * LANES:(p + 1) * LANES]
            for k in range(TOP_K):
                row = (slot * slot_rows + k * tm + r0) * COMBINE_PITCH + p
                v = v + gates[k] * yv[pl.ds(row, SUBLANES, stride=COMBINE_PITCH), :]
            acc_ref[pl.ds(r0, SUBLANES), p * LANES:(p + 1) * LANES] = v
        return c

    lax.fori_loop(0, tm // SUBLANES, group, 0, unroll=2)
    o_ref[...] = _rms(acc_ref[...], g_ref[...]) if final_norm else acc_ref[...]


def _combine(h, y_slab, pos, route_f, g, final_norm):
    t, d = h.shape
    pieces = d // LANES
    tm = min(COMBINE_TM, t)
    n_steps = t // tm
    table = lambda index: pl.BlockSpec((tm * TOP_K,), index, memory_space=pltpu.SMEM)
    return pl.pallas_call(
        functools.partial(_combine_kernel, final_norm=final_norm),
        grid=(n_steps,),
        in_specs=[table(lambda i: (i,)),
                  table(lambda i: (jnp.minimum(i + 1, n_steps - 1),)),
                  pl.BlockSpec((tm, d), lambda i: (i, 0)),
                  pl.BlockSpec((tm, LANES), lambda i: (i, 0)),
                  pl.BlockSpec((1, d), lambda i: (0, 0)),
                  pl.BlockSpec(memory_space=pl.ANY)],
        out_specs=pl.BlockSpec((tm, d), lambda i: (i, 0)),
        out_shape=jax.ShapeDtypeStruct((t, d), F32),
        scratch_shapes=[pltpu.VMEM((2 * TOP_K * tm * COMBINE_PITCH, LANES), F32), pltpu.VMEM((tm, d), F32),
                        pltpu.SemaphoreType.DMA((2,))],
        compiler_params=_cparams(("arbitrary",)),
        name="combine",
    )(pos.reshape(-1), pos.reshape(-1), h, route_f, g.reshape(1, d), y_slab)


def kernel(x, norm1_g, w_in, rel_bias, ssm_lambda_re, ssm_lambda_im, ssm_log_step, ssm_b_re, ssm_b_im, ssm_c_re, ssm_c_im, ssm_d, w_ssm_glu, b_ssm_glu, attn_out_g, ssm_out_g, w_out, norm2_g, w_router, b_router, w_gate, b_gate, w_up, b_up, w_down, b_down, norm_f_g):
    b, l, d = x.shape
    depth = w_in.shape[0]
    aw = attn_out_g.shape[-1]
    n_heads = aw // HEAD_DIM
    n_experts = w_router.shape[-1]
    t = b * l
    assert l % ATT_TQ == 0 and l >= ATT_WIN and l % S5_TC == 0
    assert (t * TOP_K) % MOE_TM == 0 and t % DISPATCH_TM == 0

    h = x.reshape(t, d).astype(F32)
    for li in range(depth):
        last = li == depth - 1
        proj = _inproj(h, norm1_g[li], w_in[li].astype(BF16), tm=min(1024, t), tn=1024)
        proj3 = proj.reshape(b, l, -1)
        mixed_a = _attention(proj3, _attn_bias_table(rel_bias[li]), attn_out_g[li], n_heads)
        wbd, cbd, avec = _s5_params(ssm_lambda_re[li], ssm_lambda_im[li], ssm_log_step[li],
                                    ssm_b_re[li], ssm_b_im[li], ssm_c_re[li], ssm_c_im[li])
        mixed_s = _s5(proj3, wbd, cbd, avec, ssm_d[li], w_ssm_glu[li].astype(BF16), b_ssm_glu[li],
                      ssm_out_g[li], u_block_index=3 * aw // (d - aw))
        h, n2_slab, route_i, route_f, cnt = _outproj(
            mixed_a.reshape(t, aw), mixed_s.reshape(t, d - aw), h, w_out[li].astype(BF16),
            norm2_g[li], w_router[li], b_router[li], tm=min(512, t))
        pos, st_e, st_t0, st_nt, pad_tile, n_used, num_st = _route_tables(
            route_i, cnt[0, :n_experts].astype(I32), t, n_experts)
        xs_slab = _dispatch(n2_slab, pos, pad_tile, d)
        ys_slab = _experts(xs_slab, (st_e, st_t0, st_nt, n_used), num_st, w_gate[li], b_gate[li],
                           w_up[li], b_up[li], w_down[li], b_down[li])
        g_fin = norm_f_g if last else jnp.ones((d,), F32)
        h = _combine(h, ys_slab, pos, route_f, g_fin, final_norm=last)
    return h.reshape(b, l, d).astype(x.dtype)
```

```python
import functools
import math

import jax
import jax.numpy as jnp
from jax import lax
from jax.experimental import pallas as pl
from jax.experimental.pallas import tpu as pltpu

F32 = jnp.float32
BF16 = jnp.bfloat16
I32 = jnp.int32

CHUNK = 64
HEAD_DIM = 128
N_BACK_CHUNKS = 8
REL_CLIP = 128
SSM_GROUP = 16
SSM_STATE = 64
TOP_K = 4
SWIGLU_LIMIT = 7.0
SWIGLU_ALPHA = 1.702
EPS = 1e-5
NEG_BIG = -1e30

LANES = 128
SUBLANES = 8
MXU_DIM = 256
VMEM_LIMIT_BYTES = 56 * 1024 * 1024

ATT_TQ = 2 * CHUNK
ATT_WIN = (N_BACK_CHUNKS + 2) * CHUNK
ATT_NSHIFT = (N_BACK_CHUNKS * CHUNK) // ATT_TQ + 1

S5_GROUPS_PER_BLOCK = MXU_DIM // SSM_GROUP
S5_BLOCK_STATES = S5_GROUPS_PER_BLOCK * SSM_STATE
S5_STRIP = 512
S5_TC = 32

MOE_TM = 256
MOE_NTS = 9
MOE_TF = 256
MOE_CHUNK_TILES = 4
COMBINE_PITCH = 24
EXPERT_PITCH = 20
DISPATCH_TM = 512
COMBINE_TM = 256


def _cparams(sem):
    return pltpu.CompilerParams(dimension_semantics=sem, vmem_limit_bytes=VMEM_LIMIT_BYTES)


def _rms(x, g):
    r = lax.rsqrt(jnp.mean(x * x, axis=-1, keepdims=True) + EPS)
    return x * r * g


def _to_slab(slab_ref, slab_row0, src_ref, src_row0, n_groups, pitch=None):
    pieces = src_ref.shape[-1] // LANES
    pitch = pitch or pieces

    def group(g, c):
        r0 = g * SUBLANES
        src0 = pl.multiple_of(src_row0 + r0, SUBLANES)
        for p in range(pieces):
            slab_ref[pl.ds((slab_row0 + r0) * pitch + p, SUBLANES, stride=pitch), :] = (
                src_ref[pl.ds(src0, SUBLANES), p * LANES:(p + 1) * LANES])
        return c

    lax.fori_loop(0, n_groups, group, 0)


def _from_slab(dst_ref, dst_row0, slab_ref, slab_row0, n_groups, pitch=None):
    pieces = dst_ref.shape[-1] // LANES
    pitch = pitch or pieces
    rows = 2 * SUBLANES

    def group(g, c):
        r0 = g * rows
        dst0 = pl.multiple_of(dst_row0 + r0, rows)
        for p in range(pieces):
            lo = slab_ref[pl.ds((slab_row0 + r0) * pitch + p, SUBLANES, stride=pitch), :]
            hi = slab_ref[pl.ds((slab_row0 + r0 + SUBLANES) * pitch + p, SUBLANES, stride=pitch), :]
            dst_ref[pl.ds(dst0, rows), p * LANES:(p + 1) * LANES] = (
                jnp.concatenate([lo, hi], axis=0).astype(dst_ref.dtype))
        return c

    lax.fori_loop(0, n_groups, group, 0)


def _inproj_kernel(x_ref, g_ref, w_ref, o_ref, nb_ref):
    @pl.when(pl.program_id(1) == 0)
    def _():
        nb_ref[...] = _rms(x_ref[...], g_ref[...]).astype(BF16)

    o_ref[...] = jnp.dot(nb_ref[...], w_ref[...], preferred_element_type=F32).astype(o_ref.dtype)


def _inproj(x2d, g, w_bf16, tm, tn):
    t, d = x2d.shape
    n = w_bf16.shape[1]
    return pl.pallas_call(
        _inproj_kernel,
        grid=(t // tm, n // tn),
        in_specs=[
            pl.BlockSpec((tm, d), lambda i, j: (i, 0)),
            pl.BlockSpec((1, d), lambda i, j: (0, 0)),
            pl.BlockSpec((d, tn), lambda i, j: (0, j)),
        ],
        out_specs=pl.BlockSpec((tm, tn), lambda i, j: (i, j)),
        out_shape=jax.ShapeDtypeStruct((t, n), BF16),
        scratch_shapes=[pltpu.VMEM((tm, d), BF16)],
        compiler_params=_cparams(("arbitrary", "arbitrary")),
        name="inproj",
    )(x2d, g.reshape(1, d), w_bf16)


def _attn_bias_table(rel_bias):
    n_heads = rel_bias.shape[0]
    period = ATT_WIN + ATT_TQ
    m = jnp.arange(period)
    diff = jnp.where(m < ATT_WIN, m, m - period)
    sh = jnp.arange(ATT_NSHIFT)[:, None]
    rel = N_BACK_CHUNKS * CHUNK - ATT_TQ * sh - diff[None, :]
    idx = jnp.clip(rel, -REL_CLIP, REL_CLIP) + REL_CLIP
    vext = rel_bias.astype(F32)[:, idx]
    flat = jnp.tile(vext, (1, 1, ATT_TQ))[..., :ATT_TQ * (period - 1)]
    tbl = flat.reshape(n_heads, ATT_NSHIFT, ATT_TQ, period - 1)[..., :ATT_WIN]
    return jnp.transpose(tbl, (1, 0, 2, 3))


def _attn_kernel(q_ref, k_ref, v_ref, b_ref, g_ref, o_ref, a_ref, *, n_heads):
    i = pl.program_id(1)
    s0 = pl.multiple_of(jnp.maximum(i * ATT_TQ - N_BACK_CHUNKS * CHUNK, 0), ATT_TQ)
    row = lax.broadcasted_iota(I32, (ATT_TQ, ATT_WIN), 0)
    col = lax.broadcasted_iota(I32, (ATT_TQ, ATT_WIN), 1)
    qc = (i * ATT_TQ + row) // CHUNK
    kc = (s0 + col) // CHUNK
    ok = (kc <= qc) & (kc >= qc - N_BACK_CHUNKS)
    scale = 1.0 / math.sqrt(HEAD_DIM)
    for h in range(n_heads):
        hs = slice(h * HEAD_DIM, (h + 1) * HEAD_DIM)
        qh = q_ref[:, hs]
        kh = k_ref[pl.ds(s0, ATT_WIN), hs]
        vh = v_ref[pl.ds(s0, ATT_WIN), hs]
        s = lax.dot_general(qh, kh, (((1,), (1,)), ((), ())), preferred_element_type=F32)
        s = jnp.where(ok, s * scale + b_ref[h], NEG_BIG)
        m = jnp.max(s, axis=-1, keepdims=True)
        p = jnp.exp(s - m)
        l = jnp.sum(p, axis=-1, keepdims=True)
        o = jnp.dot(p.astype(BF16), vh, preferred_element_type=F32)
        a_ref[:, hs] = o / l
    o_ref[...] = _rms(a_ref[...], g_ref[...]).astype(o_ref.dtype)


def _attention(proj3, bias_tbl, g, n_heads):
    b, l, _ = proj3.shape
    aw = n_heads * HEAD_DIM
    nsh = ATT_NSHIFT
    return pl.pallas_call(
        functools.partial(_attn_kernel, n_heads=n_heads),
        grid=(b, l // ATT_TQ),
        in_specs=[
            pl.BlockSpec((None, ATT_TQ, aw), lambda bi, i: (bi, i, 0)),
            pl.BlockSpec((None, l, aw), lambda bi, i: (bi, 0, 1)),
            pl.BlockSpec((None, l, aw), lambda bi, i: (bi, 0, 2)),
            pl.BlockSpec((None, n_heads, ATT_TQ, ATT_WIN),
                         lambda bi, i: (jnp.maximum(nsh - 1 - i, 0), 0, 0, 0)),
            pl.BlockSpec((1, aw), lambda bi, i: (0, 0)),
        ],
        out_specs=pl.BlockSpec((None, ATT_TQ, aw), lambda bi, i: (bi, i, 0)),
        out_shape=jax.ShapeDtypeStruct((b, l, aw), BF16),
        scratch_shapes=[pltpu.VMEM((ATT_TQ, aw), F32)],
        compiler_params=_cparams(("arbitrary", "arbitrary")),
        name="attn",
    )(proj3, proj3, proj3, bias_tbl, g.reshape(1, aw))


def _s5_params(lam_re, lam_im, log_step, b_re, b_im, c_re, c_im):
    g, p = lam_re.shape
    nb = g // S5_GROUPS_PER_BLOCK
    gl = S5_GROUPS_PER_BLOCK
    dt = jnp.exp(log_step.astype(F32))[:, None]
    lr = lam_re.astype(F32)
    li = lam_im.astype(F32)
    mag = jnp.exp(lr * dt)
    ab_re = mag * jnp.cos(li * dt)
    ab_im = mag * jnp.sin(li * dt)
    den = lr * lr + li * li
    nr = ab_re - 1.0
    ni = ab_im
    f_re = (nr * lr + ni * li) / den
    f_im = (ni * lr - nr * li) / den
    br = b_re.astype(F32)
    bi = b_im.astype(F32)
    bb_re = f_re[..., None] * br - f_im[..., None] * bi
    bb_im = f_re[..., None] * bi + f_im[..., None] * br
    eye = jnp.eye(gl, dtype=F32)
    bb = jnp.stack([bb_re, bb_im]).reshape(2, nb, gl, p, SSM_GROUP)
    wbd = jnp.einsum("ajgpc,gh->jgcahp", bb, eye).reshape(nb, gl * SSM_GROUP, 2 * gl * p)
    cc = jnp.stack([c_re.astype(F32), -c_im.astype(F32)]).reshape(2, nb, gl, SSM_GROUP, p)
    cbd = jnp.einsum("ajgcp,gh->jagphc", cc, eye).reshape(nb, 2 * gl * p, gl * SSM_GROUP)
    avec = jnp.stack([ab_re, ab_im]).reshape(2, nb, gl, p)
    avec = jnp.transpose(avec, (1, 0, 2, 3)).reshape(1, 2 * g * p)
    return wbd.astype(BF16), cbd.astype(BF16), avec


def _s5_kernel(u_ref, pm_ref, pmt_ref, wbd_ref, cbd_ref, av_ref, d_ref, wglu_ref, bglu_ref,
               g_ref, o_ref, xs_ref, hst_ref, *, n_blocks, tc, batch):
    rows = tc * batch
    sw = u_ref.shape[-1]
    bw = 2 * S5_BLOCK_STATES
    cw = S5_GROUPS_PER_BLOCK * SSM_GROUP

    @pl.when(pl.program_id(0) == 0)
    def _():
        hst_ref[...] = jnp.zeros_like(hst_ref)

    u_tm = jnp.dot(pm_ref[...], u_ref[...].reshape(rows, sw), preferred_element_type=F32)
    u_tmb = u_tm.astype(BF16)
    for j in range(n_blocks):
        xs_ref[:, j * bw:(j + 1) * bw] = jnp.dot(
            u_tmb[:, j * cw:(j + 1) * cw], wbd_ref[j], preferred_element_type=F32)

    for j in range(n_blocks):
        for s in range(S5_BLOCK_STATES // S5_STRIP):
            cr = j * bw + s * S5_STRIP
            ci = cr + S5_BLOCK_STATES
            a_re = jnp.broadcast_to(av_ref[:, cr:cr + S5_STRIP], (batch, S5_STRIP))
            a_im = jnp.broadcast_to(av_ref[:, ci:ci + S5_STRIP], (batch, S5_STRIP))

            def step(t, carry, cr=cr, ci=ci, a_re=a_re, a_im=a_im):
                h_re, h_im = carry
                r0 = pl.multiple_of(t * batch, batch)
                n_re = a_re * h_re - a_im * h_im + xs_ref[pl.ds(r0, batch), cr:cr + S5_STRIP]
                n_im = a_re * h_im + a_im * h_re + xs_ref[pl.ds(r0, batch), ci:ci + S5_STRIP]
                xs_ref[pl.ds(r0, batch), cr:cr + S5_STRIP] = n_re
                xs_ref[pl.ds(r0, batch), ci:ci + S5_STRIP] = n_im
                return n_re, n_im

            h_re, h_im = lax.fori_loop(
                0, tc, step, (hst_ref[:, cr:cr + S5_STRIP], hst_ref[:, ci:ci + S5_STRIP]), unroll=4)
            hst_ref[:, cr:cr + S5_STRIP] = h_re
            hst_ref[:, ci:ci + S5_STRIP] = h_im

    ys = [jnp.dot(xs_ref[:, j * bw:(j + 1) * bw].astype(BF16), cbd_ref[j], preferred_element_type=F32)
          for j in range(n_blocks)]
    y = jnp.concatenate(ys, axis=1) + d_ref[...] * u_tm
    y = jax.nn.gelu(y)
    z = jnp.dot(y.astype(BF16), wglu_ref[...], preferred_element_type=F32) + bglu_ref[...]
    y = y * jax.nn.sigmoid(z)
    res = _rms(y, g_ref[...]).astype(BF16)
    out_bm = jnp.dot(pmt_ref[...], res, preferred_element_type=F32).astype(o_ref.dtype)
    o_ref[...] = out_bm.reshape(batch, tc, sw)


def _s5(proj3, wbd, cbd, avec, d, wglu_bf16, bglu, g, u_block_index):
    b, l, _ = proj3.shape
    assert b == SUBLANES, "the scan keeps the batch on the sublane axis"
    n_blocks, cw, bw = wbd.shape
    sw = n_blocks * cw
    tc = S5_TC
    rows = tc * b
    r = jnp.arange(rows)
    src = (r % b) * tc + r // b
    pm = (src[:, None] == jnp.arange(rows)[None, :]).astype(BF16)
    pmt = pm.T
    const = lambda shape: pl.BlockSpec(shape, lambda c: (0,) * len(shape))
    return pl.pallas_call(
        functools.partial(_s5_kernel, n_blocks=n_blocks, tc=tc, batch=b),
        grid=(l // tc,),
        in_specs=[
            pl.BlockSpec((b, tc, sw), lambda c: (0, c, u_block_index)),
            const((rows, rows)), const((rows, rows)),
            const((n_blocks, cw, bw)), const((n_blocks, bw, cw)),
            const((1, n_blocks * bw)), const((1, sw)),
            const((sw, sw)), const((1, sw)), const((1, sw)),
        ],
        out_specs=pl.BlockSpec((b, tc, sw), lambda c: (0, c, 0)),
        out_shape=jax.ShapeDtypeStruct((b, l, sw), BF16),
        scratch_shapes=[pltpu.VMEM((rows, n_blocks * bw), F32),
                        pltpu.VMEM((b, n_blocks * bw), F32)],
        compiler_params=_cparams(("arbitrary",)),
        name="s5",
    )(proj3, pm, pmt, wbd, cbd, avec, d.reshape(1, sw), wglu_bf16, bglu.reshape(1, sw),
      g.reshape(1, sw))


def _outproj_kernel(ma_ref, ms_ref, x_ref, wo_ref, g2_ref, wr_ref, br_ref, tri_ref,
                    h_ref, n2s_ref, ri_ref, rf_ref, cnt_ref, carry_ref, n2_ref):
    aw = ma_ref.shape[-1]

    @pl.when(pl.program_id(0) == 0)
    def _():
        carry_ref[...] = jnp.zeros_like(carry_ref)

    h = (x_ref[...]
         + jnp.dot(ma_ref[...], wo_ref[:aw, :], preferred_element_type=F32)
         + jnp.dot(ms_ref[...], wo_ref[aw:, :], preferred_element_type=F32))
    h_ref[...] = h
    n2 = _rms(h, g2_ref[...])
    n2_ref[...] = n2
    _to_slab(n2s_ref, 0, n2_ref, 0, n2.shape[0] // SUBLANES)

    n_hi = n2.astype(BF16)
    n_lo = (n2 - n_hi.astype(F32)).astype(BF16)
    w = wr_ref[...]
    w_hi = w.astype(BF16)
    w_lo = (w - w_hi.astype(F32)).astype(BF16)
    logits = (jnp.dot(n_hi, w_hi, preferred_element_type=F32)
              + jnp.dot(n_lo, w_hi, preferred_element_type=F32)
              + jnp.dot(n_hi, w_lo, preferred_element_type=F32)) + br_ref[...]

    tm, ne = logits.shape
    lane = lax.broadcasted_iota(I32, (tm, ne), 1)
    work = logits
    vals, idxs = [], []
    for _ in range(TOP_K):
        m = jnp.max(work, axis=-1, keepdims=True)
        ik = jnp.min(jnp.where(work == m, lane, ne), axis=-1, keepdims=True)
        vals.append(m)
        idxs.append(ik)
        work = jnp.where(lane == ik, -jnp.inf, work)
    es = [jnp.exp(v - vals[0]) for v in vals]
    den = es[0] + es[1] + es[2] + es[3]

    hot = jnp.zeros((tm, ne), F32)
    for ik in idxs:
        hot = hot + (lane == ik).astype(F32)
    before = jnp.dot(tri_ref[...], hot.astype(BF16), preferred_element_type=F32) + carry_ref[...]
    ri = jnp.zeros((tm, ne), I32)
    rf = jnp.zeros((tm, ne), F32)
    for k in range(TOP_K):
        rank = jnp.sum(jnp.where(lane == idxs[k], before, 0.0), axis=-1, keepdims=True)
        ri = jnp.where(lane == k, idxs[k], ri)
        ri = jnp.where(lane == TOP_K + k, rank.astype(I32), ri)
        rf = jnp.where(lane == k, es[k] / den, rf)
    ri_ref[...] = ri
    rf_ref[...] = rf
    carry_ref[...] = carry_ref[...] + jnp.sum(hot, axis=0, keepdims=True)
    cnt_ref[...] = carry_ref[...]


def _outproj(ma, ms, x2d, wo_bf16, g2, w_router, b_router, tm):
    t, d = x2d.shape
    aw = ma.shape[1]
    ne = w_router.shape[1]
    pieces = d // LANES
    wr = jnp.zeros((d, LANES), F32).at[:, :ne].set(w_router.astype(F32))
    br = jnp.full((1, LANES), NEG_BIG, F32).at[0, :ne].set(b_router.astype(F32))
    tri = (jnp.arange(tm)[:, None] > jnp.arange(tm)[None, :]).astype(BF16)
    const = lambda shape: pl.BlockSpec(shape, lambda i: (0,) * len(shape))
    row = lambda w: pl.BlockSpec((tm, w), lambda i: (i, 0))
    return pl.pallas_call(
        _outproj_kernel,
        grid=(t // tm,),
        in_specs=[row(aw), row(ms.shape[1]), row(d), const((d, d)), const((1, d)),
                  const((d, LANES)), const((1, LANES)), const((tm, tm))],
        out_specs=[row(d), pl.BlockSpec((tm * pieces, LANES), lambda i: (i, 0)),
                   row(LANES), row(LANES), const((1, LANES))],
        out_shape=[jax.ShapeDtypeStruct((t, d), F32), jax.ShapeDtypeStruct((t * pieces, LANES), F32),
                   jax.ShapeDtypeStruct((t, LANES), I32), jax.ShapeDtypeStruct((t, LANES), F32),
                   jax.ShapeDtypeStruct((1, LANES), F32)],
        scratch_shapes=[pltpu.VMEM((1, LANES), F32), pltpu.VMEM((tm, d), F32)],
        compiler_params=_cparams(("arbitrary",)),
        name="outproj",
    )(ma, ms, x2d, wo_bf16, g2.reshape(1, d), wr, br, tri)


def _route_tables(route_i, counts, n_tokens, n_experts):
    idx = route_i[:, :TOP_K]
    rank = route_i[:, TOP_K:2 * TOP_K]
    n_tiles_max = n_tokens * TOP_K // MOE_TM + n_experts
    s_max = n_experts + n_tiles_max // MOE_NTS
    e_iota = jnp.arange(n_experts, dtype=I32)

    tiles_e = (counts + MOE_TM - 1) // MOE_TM
    tile_end = jnp.cumsum(tiles_e)
    tile_off = tile_end - tiles_e
    off_of = jnp.sum(jnp.where(idx[..., None] == e_iota, tile_off, 0), axis=-1)
    pos = (off_of * MOE_TM + rank).astype(I32)

    nst_e = (tiles_e + MOE_NTS - 1) // MOE_NTS
    st_incl = jnp.cumsum(nst_e)
    num_st = st_incl[-1]
    s = jnp.arange(s_max, dtype=I32)
    active = s < num_st
    s_eff = jnp.minimum(s, num_st - 1)
    e_s = jnp.minimum(jnp.sum(st_incl[None, :] <= s_eff[:, None], axis=1), n_experts - 1).astype(I32)
    onehot = e_s[:, None] == e_iota[None, :]
    pick = lambda v: jnp.sum(jnp.where(onehot, v[None, :], 0), axis=1)
    local = s_eff - (pick(st_incl) - pick(nst_e))
    tile0 = (pick(tile_off) + MOE_NTS * local).astype(I32)
    nt = jnp.where(active, jnp.minimum(MOE_NTS, pick(tiles_e) - MOE_NTS * local), 0).astype(I32)

    tile_ids = jnp.arange(n_tiles_max, dtype=I32)
    is_last = jnp.any((tile_ids[:, None] == tile_end[None, :] - 1) & (tiles_e[None, :] > 0), axis=1)
    pad_tile = (is_last | (tile_ids >= tile_end[-1])).astype(I32)
    return pos, e_s, tile0, nt, pad_tile, tile_end[-1:].astype(I32), num_st.astype(I32)


def _dispatch_kernel(pad_tile, pos_ref, n2_ref, xs_hbm, zero_ref, sem_z, sem_r, *,
                     tokens_per_step, pieces):
    i = pl.program_id(0)
    tile_rows = MOE_TM * pieces
    n_tiles = pad_tile.shape[0]

    @pl.when(i == 0)
    def _zero_fill():
        zero_ref[...] = jnp.zeros_like(zero_ref)

        def fill(tl):
            start = pl.multiple_of(tl * tile_rows, tile_rows)
            return pltpu.make_async_copy(zero_ref, xs_hbm.at[pl.ds(start, tile_rows)], sem_z)

        def issue(tl, c):
            @pl.when(pad_tile[tl] > 0)
            def _():
                fill(tl).start()
            return c

        def drain(tl, c):
            @pl.when(pad_tile[tl] > 0)
            def _():
                fill(tl).wait()
            return c

        lax.fori_loop(0, n_tiles, issue, 0)
        lax.fori_loop(0, n_tiles, drain, 0)

    def row_copy(tok, dst):
        return pltpu.make_async_copy(
            n2_ref.at[pl.ds(pl.multiple_of(tok * pieces, pieces), pieces)],
            xs_hbm.at[pl.ds(pl.multiple_of(dst * pieces, pieces), pieces)], sem_r)

    def token(tok, c):
        for k in range(TOP_K):
            row_copy(tok, pos_ref[tok * TOP_K + k]).start(priority=k % 2)
        return c

    lax.fori_loop(0, tokens_per_step, token, 0, unroll=4)

    def drain(m, c):
        for _ in range(LANES):
            row_copy(0, 0).wait()
        return c

    lax.fori_loop(0, tokens_per_step * TOP_K // LANES, drain, 0)


def _dispatch(n2_slab, pos, pad_tile, d):
    pieces = d // LANES
    n_tokens = n2_slab.shape[0] // pieces
    n_tiles = pad_tile.shape[0]
    tm = min(DISPATCH_TM, n_tokens)
    grid_spec = pltpu.PrefetchScalarGridSpec(
        num_scalar_prefetch=1,
        grid=(n_tokens // tm,),
        in_specs=[pl.BlockSpec((tm * TOP_K,), lambda i, pad: (i,), memory_space=pltpu.SMEM),
                  pl.BlockSpec((tm * pieces, LANES), lambda i, pad: (i, 0))],
        out_specs=pl.BlockSpec(memory_space=pl.ANY),
        scratch_shapes=[pltpu.VMEM((MOE_TM * pieces, LANES), F32),
                        pltpu.SemaphoreType.DMA(()), pltpu.SemaphoreType.DMA(())],
    )
    return pl.pallas_call(
        functools.partial(_dispatch_kernel, tokens_per_step=tm, pieces=pieces),
        grid_spec=grid_spec,
        out_shape=jax.ShapeDtypeStruct((n_tiles * MOE_TM * pieces, LANES), F32),
        compiler_params=_cparams(("arbitrary",)),
        name="dispatch",
    )(pad_tile, pos.reshape(-1), n2_slab)


def _experts_kernel(st_e, st_t0, st_nt, n_used, xs_hbm, wg_ref, wu_ref, wd_ref, bg_ref, bu_ref, bd_ref,
                    y_hbm, tbuf, xs_b, acc, wgb, wub, wdb, sem_g, sem_s, *, n_j, n_tiles):
    s = pl.program_id(0)
    j = pl.program_id(1)
    nt = st_nt[s]
    t0 = st_t0[s]
    pieces = acc.shape[-1] // LANES
    slot_rows = MOE_TM * EXPERT_PITCH

    def hbm_tile(ref, tl):
        return ref.at[pl.ds(pl.multiple_of(tl * MOE_TM, MOE_TM), MOE_TM)]

    def vmem_tile(q):
        slot = tbuf.at[pl.ds(pl.multiple_of((q & 1) * slot_rows, slot_rows), slot_rows)]
        return slot.reshape(MOE_TM, EXPERT_PITCH, LANES).at[:, pl.ds(0, pieces), :]

    def tile_in(q):
        return pltpu.make_async_copy(hbm_tile(xs_hbm, t0 + q), vmem_tile(q), sem_g.at[q & 1])

    def tile_out(q, tl):
        return pltpu.make_async_copy(vmem_tile(q), hbm_tile(y_hbm, tl), sem_s.at[q & 1])

    def for_tiles(lo, hi, fn):
        def body(q, c):
            fn(q)
            return c
        lax.fori_loop(lo, hi, body, 0)

    @pl.when((s == 0) & (j == 0))
    def _zero_tail():
        tbuf[pl.ds(0, slot_rows), :] = jnp.zeros((slot_rows, LANES), F32)
        for_tiles(n_used[0], n_tiles, lambda tl: tile_out(0, tl).start())
        for_tiles(n_used[0], n_tiles, lambda tl: tile_out(0, tl).wait())

    @pl.when(nt > 0)
    def _active():
        @pl.when(j == 0)
        def _load():
            tile_in(0).start()

            def load(q):
                @pl.when(q + 1 < nt)
                def _():
                    tile_in(q + 1).start()

                tile_in(q).wait()
                _from_slab(xs_b, q * MOE_TM, tbuf, (q & 1) * MOE_TM, MOE_TM // (2 * SUBLANES), EXPERT_PITCH)
                acc[pl.ds(pl.multiple_of(q * MOE_TM, MOE_TM), MOE_TM), :] = jnp.broadcast_to(
                    bd_ref[...], (MOE_TM, acc.shape[-1]))

            for_tiles(0, nt, load)

        wgb[...] = wg_ref[...].astype(BF16)
        wub[...] = wu_ref[...].astype(BF16)
        wdb[...] = wd_ref[...].astype(BF16)

        def ffn(tile0, n_tiles_chunk):
            rows = n_tiles_chunk * MOE_TM
            r0 = pl.multiple_of(tile0 * MOE_TM, MOE_TM)
            xt = xs_b[pl.ds(r0, rows), :]
            g = jnp.dot(xt, wgb[...], preferred_element_type=F32) + bg_ref[...]
            u = jnp.dot(xt, wub[...], preferred_element_type=F32) + bu_ref[...]
            g = jnp.minimum(g, SWIGLU_LIMIT)
            u = jnp.clip(u, -SWIGLU_LIMIT, SWIGLU_LIMIT)
            hdn = g * jax.nn.sigmoid(SWIGLU_ALPHA * g) * (u + 1.0)
            acc[pl.ds(r0, rows), :] += jnp.dot(hdn.astype(BF16), wdb[...], preferred_element_type=F32)

        big = MOE_CHUNK_TILES
        for_tiles(0, nt // big, lambda c: ffn(c * big, big))
        done = (nt // big) * big
        piece = big // 2
        while piece >= 1:
            @pl.when((nt & piece) != 0)
            def _(done=done, piece=piece):
                ffn(done, piece)

            done = done + (nt & piece)
            piece //= 2

        @pl.when(j == n_j - 1)
        def _store():
            def store(q):
                @pl.when(q >= 2)
                def _():
                    tile_out(q - 2, t0 + q - 2).wait()

                _to_slab(tbuf, (q & 1) * MOE_TM, acc, q * MOE_TM, MOE_TM // SUBLANES, EXPERT_PITCH)
                tile_out(q, t0 + q).start()

            for_tiles(0, nt, store)

            @pl.when(nt >= 2)
            def _():
                tile_out(nt - 2, t0 + nt - 2).wait()

            tile_out(nt - 1, t0 + nt - 1).wait()


def _experts(xs_slab, tables, num_super_tiles, w_gate, b_gate, w_up, b_up, w_down, b_down):
    st_e, st_t0, st_nt, n_used = tables
    ne, d, f = w_gate.shape
    pieces = d // LANES
    n_tiles = xs_slab.shape[0] // (MOE_TM * pieces)
    n_j = f // MOE_TF
    super_rows = MOE_NTS * MOE_TM

    def w_in_map(s, j, st_e, st_t0, st_nt, n_used):
        return (st_e[s], 0, j)

    def w_down_map(s, j, st_e, st_t0, st_nt, n_used):
        return (st_e[s], j, 0)

    def b_down_map(s, j, st_e, st_t0, st_nt, n_used):
        return (st_e[s], 0, 0)

    any_spec = pl.BlockSpec(memory_space=pl.ANY)
    grid_spec = pltpu.PrefetchScalarGridSpec(
        num_scalar_prefetch=4,
        grid=(num_super_tiles, n_j),
        in_specs=[
            any_spec,
            pl.BlockSpec((None, d, MOE_TF), w_in_map),
            pl.BlockSpec((None, d, MOE_TF), w_in_map),
            pl.BlockSpec((None, MOE_TF, d), w_down_map),
            pl.BlockSpec((None, 1, MOE_TF), w_in_map),
            pl.BlockSpec((None, 1, MOE_TF), w_in_map),
            pl.BlockSpec((None, 1, d), b_down_map),
        ],
        out_specs=any_spec,
        scratch_shapes=[
            pltpu.VMEM((2 * MOE_TM * EXPERT_PITCH, LANES), F32),
            pltpu.VMEM((super_rows, d), BF16),
            pltpu.VMEM((super_rows, d), F32),
            pltpu.VMEM((d, MOE_TF), BF16),
            pltpu.VMEM((d, MOE_TF), BF16),
            pltpu.VMEM((MOE_TF, d), BF16),
            pltpu.SemaphoreType.DMA((2,)),
            pltpu.SemaphoreType.DMA((2,)),
        ],
    )
    return pl.pallas_call(
        functools.partial(_experts_kernel, n_j=n_j, n_tiles=n_tiles),
        grid_spec=grid_spec,
        out_shape=jax.ShapeDtypeStruct((n_tiles * MOE_TM, pieces, LANES), F32),
        compiler_params=_cparams(("arbitrary", "arbitrary")),
        name="experts",
    )(st_e, st_t0, st_nt, n_used, xs_slab.reshape(n_tiles * MOE_TM, pieces, LANES),
      w_gate, w_up, w_down, b_gate.reshape(ne, 1, f), b_up.reshape(ne, 1, f),
      b_down.reshape(ne, 1, d)).reshape(xs_slab.shape)


def _combine_kernel(pos_ref, pos_next_ref, h_ref, rf_ref, g_ref, y_hbm, o_ref, yv, acc_ref, sem, *,
                    final_norm):
    i = pl.program_id(0)
    tm, d = h_ref.shape
    pieces = d // LANES
    slot_rows = TOP_K * tm
    slot = i & 1

    def row_copy(src, dst, sl):
        return pltpu.make_async_copy(
            y_hbm.at[pl.ds(pl.multiple_of(src * pieces, pieces), pieces)],
            yv.at[pl.ds(pl.multiple_of((sl * slot_rows + dst) * COMBINE_PITCH, SUBLANES), pieces)],
            sem.at[sl])

    def gather(table, sl):
        def token(tok, c):
            for k in range(TOP_K):
                row_copy(table[tok * TOP_K + k], k * tm + tok, sl).start(priority=k % 2)
            return c

        lax.fori_loop(0, tm, token, 0, unroll=4)

    @pl.when(i == 0)
    def _():
        gather(pos_ref, 0)

    @pl.when(i + 1 < pl.num_programs(0))
    def _():
        gather(pos_next_ref, 1 - slot)

    def drain(m, c):
        for _ in range(LANES):
            row_copy(0, 0, slot).wait()
        return c

    lax.fori_loop(0, slot_rows // LANES, drain, 0)

    def group(g, c):
        r0 = pl.multiple_of(g * SUBLANES, SUBLANES)
        rf = rf_ref[pl.ds(r0, SUBLANES), :]
        gates = [jnp.broadcast_to(rf[:, k:k + 1], (SUBLANES, LANES)) for k in range(TOP_K)]
        for p in range(pieces):
            v = h_ref[pl.ds(r0, SUBLANES), p * LANES:(p + 1) * LANES]
            for k in range(TOP_K):
                row = (slot * slot_rows + k * tm + r0) * COMBINE_PITCH + p
                v = v + gates[k] * yv[pl.ds(row, SUBLANES, stride=COMBINE_PITCH), :]
            acc_ref[pl.ds(r0, SUBLANES), p * LANES:(p + 1) * LANES] = v
        return c

    lax.fori_loop(0, tm // SUBLANES, group, 0, unroll=2)
    o_ref[...] = _rms(acc_ref[...], g_ref[...]) if final_norm else acc_ref[...]


def _combine(h, y_slab, pos, route_f, g, final_norm):
    t, d = h.shape
    pieces = d // LANES
    tm = min(COMBINE_TM, t)
    n_steps = t // tm
    table = lambda index: pl.BlockSpec((tm * TOP_K,), index, memory_space=pltpu.SMEM)
    return pl.pallas_call(
        functools.partial(_combine_kernel, final_norm=final_norm),
        grid=(n_steps,),
        in_specs=[table(lambda i: (i,)),
                  table(lambda i: (jnp.minimum(i + 1, n_steps - 1),)),
                  pl.BlockSpec((tm, d), lambda i: (i, 0)),
                  pl.BlockSpec((tm, LANES), lambda i: (i, 0)),
                  pl.BlockSpec((1, d), lambda i: (0, 0)),
                  pl.BlockSpec(memory_space=pl.ANY)],
        out_specs=pl.BlockSpec((tm, d), lambda i: (i, 0)),
        out_shape=jax.ShapeDtypeStruct((t, d), F32),
        scratch_shapes=[pltpu.VMEM((2 * TOP_K * tm * COMBINE_PITCH, LANES), F32), pltpu.VMEM((tm, d), F32),
                        pltpu.SemaphoreType.DMA((2,))],
        compiler_params=_cparams(("arbitrary",)),
        name="combine",
    )(pos.reshape(-1), pos.reshape(-1), h, route_f, g.reshape(1, d), y_slab)


def kernel(x, norm1_g, w_in, rel_bias, ssm_lambda_re, ssm_lambda_im, ssm_log_step, ssm_b_re, ssm_b_im, ssm_c_re, ssm_c_im, ssm_d, w_ssm_glu, b_ssm_glu, attn_out_g, ssm_out_g, w_out, norm2_g, w_router, b_router, w_gate, b_gate, w_up, b_up, w_down, b_down, norm_f_g):
    b, l, d = x.shape
    depth = w_in.shape[0]
    aw = attn_out_g.shape[-1]
    n_heads = aw // HEAD_DIM
    n_experts = w_router.shape[-1]
    t = b * l
    assert l % ATT_TQ == 0 and l >= ATT_WIN and l % S5_TC == 0
    assert (t * TOP_K) % MOE_TM == 0 and t % DISPATCH_TM == 0

    h = x.reshape(t, d).astype(F32)
    for li in range(depth):
        last = li == depth - 1
        proj = _inproj(h, norm1_g[li], w_in[li].astype(BF16), tm=min(1024, t), tn=1024)
        proj3 = proj.reshape(b, l, -1)
        mixed_a = _attention(proj3, _attn_bias_table(rel_bias[li]), attn_out_g[li], n_heads)
        wbd, cbd, avec = _s5_params(ssm_lambda_re[li], ssm_lambda_im[li], ssm_log_step[li],
                                    ssm_b_re[li], ssm_b_im[li], ssm_c_re[li], ssm_c_im[li])
        mixed_s = _s5(proj3, wbd, cbd, avec, ssm_d[li], w_ssm_glu[li].astype(BF16), b_ssm_glu[li],
                      ssm_out_g[li], u_block_index=3 * aw // (d - aw))
        h, n2_slab, route_i, route_f, cnt = _outproj(
            mixed_a.reshape(t, aw), mixed_s.reshape(t, d - aw), h, w_out[li].astype(BF16),
            norm2_g[li], w_router[li], b_router[li], tm=min(512, t))
        pos, st_e, st_t0, st_nt, pad_tile, n_used, num_st = _route_tables(
            route_i, cnt[0, :n_experts].astype(I32), t, n_experts)
        xs_slab = _dispatch(n2_slab, pos, pad_tile, d)
        ys_slab = _experts(xs_slab, (st_e, st_t0, st_nt, n_used), num_st, w_gate[li], b_gate[li],
                           w_up[li], b_up[li], w_down[li], b_down[li])
        g_fin = norm_f_g if last else jnp.ones((d,), F32)
        h = _combine(h, ys_slab, pos, route_f, g_fin, final_norm=last)
    return h.reshape(b, l, d).astype(x.dtype)
```

```python
import functools
import math

import jax
import jax.numpy as jnp
from jax import lax
from jax.experimental import pallas as pl
from jax.experimental.pallas import tpu as pltpu

F32 = jnp.float32
BF16 = jnp.bfloat16
I32 = jnp.int32

CHUNK = 64
HEAD_DIM = 128
N_BACK_CHUNKS = 8
REL_CLIP = 128
SSM_GROUP = 16
SSM_STATE = 64
TOP_K = 4
SWIGLU_LIMIT = 7.0
SWIGLU_ALPHA = 1.702
EPS = 1e-5
NEG_BIG = -1e30

LANES = 128
SUBLANES = 8
MXU_DIM = 256
VMEM_LIMIT_BYTES = 56 * 1024 * 1024

ATT_TQ = 2 * CHUNK
ATT_WIN = (N_BACK_CHUNKS + 2) * CHUNK
ATT_NSHIFT = (N_BACK_CHUNKS * CHUNK) // ATT_TQ + 1

S5_GROUPS_PER_BLOCK = MXU_DIM // SSM_GROUP
S5_BLOCK_STATES = S5_GROUPS_PER_BLOCK * SSM_STATE
S5_STRIP = 512
S5_TC = 32

MOE_TM = 256
MOE_NTS = 9
MOE_TF = 256
MOE_CHUNK_TILES = 4
COMBINE_PITCH = 20
EXPERT_PITCH = 20
DISPATCH_TM = 512
COMBINE_TM = 256


def _cparams(sem):
    return pltpu.CompilerParams(dimension_semantics=sem, vmem_limit_bytes=VMEM_LIMIT_BYTES)


def _rms(x, g):
    r = lax.rsqrt(jnp.mean(x * x, axis=-1, keepdims=True) + EPS)
    return x * r * g


def _to_slab(slab_ref, slab_row0, src_ref, src_row0, n_groups, pitch=None):
    pieces = src_ref.shape[-1] // LANES
    pitch = pitch or pieces

    def group(g, c):
        r0 = g * SUBLANES
        src0 = pl.multiple_of(src_row0 + r0, SUBLANES)
        for p in range(pieces):
            slab_ref[pl.ds((slab_row0 + r0) * pitch + p, SUBLANES, stride=pitch), :] = (
                src_ref[pl.ds(src0, SUBLANES), p * LANES:(p + 1) * LANES])
        return c

    lax.fori_loop(0, n_groups, group, 0)


def _from_slab(dst_ref, dst_row0, slab_ref, slab_row0, n_groups, pitch=None):
    pieces = dst_ref.shape[-1] // LANES
    pitch = pitch or pieces
    rows = 2 * SUBLANES

    def group(g, c):
        r0 = g * rows
        dst0 = pl.multiple_of(dst_row0 + r0, rows)
        for p in range(pieces):
            lo = slab_ref[pl.ds((slab_row0 + r0) * pitch + p, SUBLANES, stride=pitch), :]
            hi = slab_ref[pl.ds((slab_row0 + r0 + SUBLANES) * pitch + p, SUBLANES, stride=pitch), :]
            dst_ref[pl.ds(dst0, rows), p * LANES:(p + 1) * LANES] = (
                jnp.concatenate([lo, hi], axis=0).astype(dst_ref.dtype))
        return c

    lax.fori_loop(0, n_groups, group, 0)


def _inproj_kernel(x_ref, g_ref, w_ref, o_ref, nb_ref):
    @pl.when(pl.program_id(1) == 0)
    def _():
        nb_ref[...] = _rms(x_ref[...], g_ref[...]).astype(BF16)

    o_ref[...] = jnp.dot(nb_ref[...], w_ref[...], preferred_element_type=F32).astype(o_ref.dtype)


def _inproj(x2d, g, w_bf16, tm, tn):
    t, d = x2d.shape
    n = w_bf16.shape[1]
    return pl.pallas_call(
        _inproj_kernel,
        grid=(t // tm, n // tn),
        in_specs=[
            pl.BlockSpec((tm, d), lambda i, j: (i, 0)),
            pl.BlockSpec((1, d), lambda i, j: (0, 0)),
            pl.BlockSpec((d, tn), lambda i, j: (0, j)),
        ],
        out_specs=pl.BlockSpec((tm, tn), lambda i, j: (i, j)),
        out_shape=jax.ShapeDtypeStruct((t, n), BF16),
        scratch_shapes=[pltpu.VMEM((tm, d), BF16)],
        compiler_params=_cparams(("arbitrary", "arbitrary")),
        name="inproj",
    )(x2d, g.reshape(1, d), w_bf16)


def _attn_bias_table(rel_bias):
    n_heads = rel_bias.shape[0]
    period = ATT_WIN + ATT_TQ
    m = jnp.arange(period)
    diff = jnp.where(m < ATT_WIN, m, m - period)
    sh = jnp.arange(ATT_NSHIFT)[:, None]
    rel = N_BACK_CHUNKS * CHUNK - ATT_TQ * sh - diff[None, :]
    idx = jnp.clip(rel, -REL_CLIP, REL_CLIP) + REL_CLIP
    vext = rel_bias.astype(F32)[:, idx]
    flat = jnp.tile(vext, (1, 1, ATT_TQ))[..., :ATT_TQ * (period - 1)]
    tbl = flat.reshape(n_heads, ATT_NSHIFT, ATT_TQ, period - 1)[..., :ATT_WIN]
    return jnp.transpose(tbl, (1, 0, 2, 3))


def _attn_kernel(q_ref, k_ref, v_ref, b_ref, g_ref, o_ref, a_ref, *, n_heads):
    i = pl.program_id(1)
    s0 = pl.multiple_of(jnp.maximum(i * ATT_TQ - N_BACK_CHUNKS * CHUNK, 0), ATT_TQ)
    row = lax.broadcasted_iota(I32, (ATT_TQ, ATT_WIN), 0)
    col = lax.broadcasted_iota(I32, (ATT_TQ, ATT_WIN), 1)
    qc = (i * ATT_TQ + row) // CHUNK
    kc = (s0 + col) // CHUNK
    ok = (kc <= qc) & (kc >= qc - N_BACK_CHUNKS)
    scale = 1.0 / math.sqrt(HEAD_DIM)
    for h in range(n_heads):
        hs = slice(h * HEAD_DIM, (h + 1) * HEAD_DIM)
        qh = q_ref[:, hs]
        kh = k_ref[pl.ds(s0, ATT_WIN), hs]
        vh = v_ref[pl.ds(s0, ATT_WIN), hs]
        s = lax.dot_general(qh, kh, (((1,), (1,)), ((), ())), preferred_element_type=F32)
        s = jnp.where(ok, s * scale + b_ref[h], NEG_BIG)
        m = jnp.max(s, axis=-1, keepdims=True)
        p = jnp.exp(s - m)
        l = jnp.sum(p, axis=-1, keepdims=True)
        o = jnp.dot(p.astype(BF16), vh, preferred_element_type=F32)
        a_ref[:, hs] = o / l
    o_ref[...] = _rms(a_ref[...], g_ref[...]).astype(o_ref.dtype)


def _attention(proj3, bias_tbl, g, n_heads):
    b, l, _ = proj3.shape
    aw = n_heads * HEAD_DIM
    nsh = ATT_NSHIFT
    return pl.pallas_call(
        functools.partial(_attn_kernel, n_heads=n_heads),
        grid=(b, l // ATT_TQ),
        in_specs=[
            pl.BlockSpec((None, ATT_TQ, aw), lambda bi, i: (bi, i, 0)),
            pl.BlockSpec((None, l, aw), lambda bi, i: (bi, 0, 1)),
            pl.BlockSpec((None, l, aw), lambda bi, i: (bi, 0, 2)),
            pl.BlockSpec((None, n_heads, ATT_TQ, ATT_WIN),
                         lambda bi, i: (jnp.maximum(nsh - 1 - i, 0), 0, 0, 0)),
            pl.BlockSpec((1, aw), lambda bi, i: (0, 0)),
        ],
        out_specs=pl.BlockSpec((None, ATT_TQ, aw), lambda bi, i: (bi, i, 0)),
        out_shape=jax.ShapeDtypeStruct((b, l, aw), BF16),
        scratch_shapes=[pltpu.VMEM((ATT_TQ, aw), F32)],
        compiler_params=_cparams(("arbitrary", "arbitrary")),
        name="attn",
    )(proj3, proj3, proj3, bias_tbl, g.reshape(1, aw))


def _s5_params(lam_re, lam_im, log_step, b_re, b_im, c_re, c_im):
    g, p = lam_re.shape
    nb = g // S5_GROUPS_PER_BLOCK
    gl = S5_GROUPS_PER_BLOCK
    dt = jnp.exp(log_step.astype(F32))[:, None]
    lr = lam_re.astype(F32)
    li = lam_im.astype(F32)
    mag = jnp.exp(lr * dt)
    ab_re = mag * jnp.cos(li * dt)
    ab_im = mag * jnp.sin(li * dt)
    den = lr * lr + li * li
    nr = ab_re - 1.0
    ni = ab_im
    f_re = (nr * lr + ni * li) / den
    f_im = (ni * lr - nr * li) / den
    br = b_re.astype(F32)
    bi = b_im.astype(F32)
    bb_re = f_re[..., None] * br - f_im[..., None] * bi
    bb_im = f_re[..., None] * bi + f_im[..., None] * br
    eye = jnp.eye(gl, dtype=F32)
    bb = jnp.stack([bb_re, bb_im]).reshape(2, nb, gl, p, SSM_GROUP)
    wbd = jnp.einsum("ajgpc,gh->jgcahp", bb, eye).reshape(nb, gl * SSM_GROUP, 2 * gl * p)
    cc = jnp.stack([c_re.astype(F32), -c_im.astype(F32)]).reshape(2, nb, gl, SSM_GROUP, p)
    cbd = jnp.einsum("ajgcp,gh->jagphc", cc, eye).reshape(nb, 2 * gl * p, gl * SSM_GROUP)
    avec = jnp.stack([ab_re, ab_im]).reshape(2, nb, gl, p)
    avec = jnp.transpose(avec, (1, 0, 2, 3)).reshape(1, 2 * g * p)
    return wbd.astype(BF16), cbd.astype(BF16), avec


def _s5_kernel(u_ref, pm_ref, pmt_ref, wbd_ref, cbd_ref, av_ref, d_ref, wglu_ref, bglu_ref,
               g_ref, o_ref, xs_ref, hst_ref, *, n_blocks, tc, batch):
    rows = tc * batch
    sw = u_ref.shape[-1]
    bw = 2 * S5_BLOCK_STATES
    cw = S5_GROUPS_PER_BLOCK * SSM_GROUP

    @pl.when(pl.program_id(0) == 0)
    def _():
        hst_ref[...] = jnp.zeros_like(hst_ref)

    u_tm = jnp.dot(pm_ref[...], u_ref[...].reshape(rows, sw), preferred_element_type=F32)
    u_tmb = u_tm.astype(BF16)
    for j in range(n_blocks):
        xs_ref[:, j * bw:(j + 1) * bw] = jnp.dot(
            u_tmb[:, j * cw:(j + 1) * cw], wbd_ref[j], preferred_element_type=F32)

    for j in range(n_blocks):
        for s in range(S5_BLOCK_STATES // S5_STRIP):
            cr = j * bw + s * S5_STRIP
            ci = cr + S5_BLOCK_STATES
            a_re = jnp.broadcast_to(av_ref[:, cr:cr + S5_STRIP], (batch, S5_STRIP))
            a_im = jnp.broadcast_to(av_ref[:, ci:ci + S5_STRIP], (batch, S5_STRIP))

            def step(t, carry, cr=cr, ci=ci, a_re=a_re, a_im=a_im):
                h_re, h_im = carry
                r0 = pl.multiple_of(t * batch, batch)
                n_re = a_re * h_re - a_im * h_im + xs_ref[pl.ds(r0, batch), cr:cr + S5_STRIP]
                n_im = a_re * h_im + a_im * h_re + xs_ref[pl.ds(r0, batch), ci:ci + S5_STRIP]
                xs_ref[pl.ds(r0, batch), cr:cr + S5_STRIP] = n_re
                xs_ref[pl.ds(r0, batch), ci:ci + S5_STRIP] = n_im
                return n_re, n_im

            h_re, h_im = lax.fori_loop(
                0, tc, step, (hst_ref[:, cr:cr + S5_STRIP], hst_ref[:, ci:ci + S5_STRIP]), unroll=4)
            hst_ref[:, cr:cr + S5_STRIP] = h_re
            hst_ref[:, ci:ci + S5_STRIP] = h_im

    ys = [jnp.dot(xs_ref[:, j * bw:(j + 1) * bw].astype(BF16), cbd_ref[j], preferred_element_type=F32)
          for j in range(n_blocks)]
    y = jnp.concatenate(ys, axis=1) + d_ref[...] * u_tm
    y = jax.nn.gelu(y)
    z = jnp.dot(y.astype(BF16), wglu_ref[...], preferred_element_type=F32) + bglu_ref[...]
    y = y * jax.nn.sigmoid(z)
    res = _rms(y, g_ref[...]).astype(BF16)
    out_bm = jnp.dot(pmt_ref[...], res, preferred_element_type=F32).astype(o_ref.dtype)
    o_ref[...] = out_bm.reshape(batch, tc, sw)


def _s5(proj3, wbd, cbd, avec, d, wglu_bf16, bglu, g, u_block_index):
    b, l, _ = proj3.shape
    assert b == SUBLANES, "the scan keeps the batch on the sublane axis"
    n_blocks, cw, bw = wbd.shape
    sw = n_blocks * cw
    tc = S5_TC
    rows = tc * b
    r = jnp.arange(rows)
    src = (r % b) * tc + r // b
    pm = (src[:, None] == jnp.arange(rows)[None, :]).astype(BF16)
    pmt = pm.T
    const = lambda shape: pl.BlockSpec(shape, lambda c: (0,) * len(shape))
    return pl.pallas_call(
        functools.partial(_s5_kernel, n_blocks=n_blocks, tc=tc, batch=b),
        grid=(l // tc,),
        in_specs=[
            pl.BlockSpec((b, tc, sw), lambda c: (0, c, u_block_index)),
            const((rows, rows)), const((rows, rows)),
            const((n_blocks, cw, bw)), const((n_blocks, bw, cw)),
            const((1, n_blocks * bw)), const((1, sw)),
            const((sw, sw)), const((1, sw)), const((1, sw)),
        ],
        out_specs=pl.BlockSpec((b, tc, sw), lambda c: (0, c, 0)),
        out_shape=jax.ShapeDtypeStruct((b, l, sw), BF16),
        scratch_shapes=[pltpu.VMEM((rows, n_blocks * bw), F32),
                        pltpu.VMEM((b, n_blocks * bw), F32)],
        compiler_params=_cparams(("arbitrary",)),
        name="s5",
    )(proj3, pm, pmt, wbd, cbd, avec, d.reshape(1, sw), wglu_bf16, bglu.reshape(1, sw),
      g.reshape(1, sw))


def _outproj_kernel(ma_ref, ms_ref, x_ref, wo_ref, g2_ref, wr_ref, br_ref, tri_ref,
                    h_ref, n2s_ref, ri_ref, rf_ref, cnt_ref, carry_ref, n2_ref):
    aw = ma_ref.shape[-1]

    @pl.when(pl.program_id(0) == 0)
    def _():
        carry_ref[...] = jnp.zeros_like(carry_ref)

    h = (x_ref[...]
         + jnp.dot(ma_ref[...], wo_ref[:aw, :], preferred_element_type=F32)
         + jnp.dot(ms_ref[...], wo_ref[aw:, :], preferred_element_type=F32))
    h_ref[...] = h
    n2 = _rms(h, g2_ref[...])
    n2_ref[...] = n2
    _to_slab(n2s_ref, 0, n2_ref, 0, n2.shape[0] // SUBLANES)

    n_hi = n2.astype(BF16)
    n_lo = (n2 - n_hi.astype(F32)).astype(BF16)
    w = wr_ref[...]
    w_hi = w.astype(BF16)
    w_lo = (w - w_hi.astype(F32)).astype(BF16)
    logits = (jnp.dot(n_hi, w_hi, preferred_element_type=F32)
              + jnp.dot(n_lo, w_hi, preferred_element_type=F32)
              + jnp.dot(n_hi, w_lo, preferred_element_type=F32)) + br_ref[...]

    tm, ne = logits.shape
    lane = lax.broadcasted_iota(I32, (tm, ne), 1)
    work = logits
    vals, idxs = [], []
    for _ in range(TOP_K):
        m = jnp.max(work, axis=-1, keepdims=True)
        ik = jnp.min(jnp.where(work == m, lane, ne), axis=-1, keepdims=True)
        vals.append(m)
        idxs.append(ik)
        work = jnp.where(lane == ik, -jnp.inf, work)
    es = [jnp.exp(v - vals[0]) for v in vals]
    den = es[0] + es[1] + es[2] + es[3]

    hot = jnp.zeros((tm, ne), F32)
    for ik in idxs:
        hot = hot + (lane == ik).astype(F32)
    before = jnp.dot(tri_ref[...], hot.astype(BF16), preferred_element_type=F32) + carry_ref[...]
    ri = jnp.zeros((tm, ne), I32)
    rf = jnp.zeros((tm, ne), F32)
    for k in range(TOP_K):
        rank = jnp.sum(jnp.where(lane == idxs[k], before, 0.0), axis=-1, keepdims=True)
        ri = jnp.where(lane == k, idxs[k], ri)
        ri = jnp.where(lane == TOP_K + k, rank.astype(I32), ri)
        rf = jnp.where(lane == k, es[k] / den, rf)
    ri_ref[...] = ri
    rf_ref[...] = rf
    carry_ref[...] = carry_ref[...] + jnp.sum(hot, axis=0, keepdims=True)
    cnt_ref[...] = carry_ref[...]


def _outproj(ma, ms, x2d, wo_bf16, g2, w_router, b_router, tm):
    t, d = x2d.shape
    aw = ma.shape[1]
    ne = w_router.shape[1]
    pieces = d // LANES
    wr = jnp.zeros((d, LANES), F32).at[:, :ne].set(w_router.astype(F32))
    br = jnp.full((1, LANES), NEG_BIG, F32).at[0, :ne].set(b_router.astype(F32))
    tri = (jnp.arange(tm)[:, None] > jnp.arange(tm)[None, :]).astype(BF16)
    const = lambda shape: pl.BlockSpec(shape, lambda i: (0,) * len(shape))
    row = lambda w: pl.BlockSpec((tm, w), lambda i: (i, 0))
    return pl.pallas_call(
        _outproj_kernel,
        grid=(t // tm,),
        in_specs=[row(aw), row(ms.shape[1]), row(d), const((d, d)), const((1, d)),
                  const((d, LANES)), const((1, LANES)), const((tm, tm))],
        out_specs=[row(d), pl.BlockSpec((tm * pieces, LANES), lambda i: (i, 0)),
                   row(LANES), row(LANES), const((1, LANES))],
        out_shape=[jax.ShapeDtypeStruct((t, d), F32), jax.ShapeDtypeStruct((t * pieces, LANES), F32),
                   jax.ShapeDtypeStruct((t, LANES), I32), jax.ShapeDtypeStruct((t, LANES), F32),
                   jax.ShapeDtypeStruct((1, LANES), F32)],
        scratch_shapes=[pltpu.VMEM((1, LANES), F32), pltpu.VMEM((tm, d), F32)],
        compiler_params=_cparams(("arbitrary",)),
        name="outproj",
    )(ma, ms, x2d, wo_bf16, g2.reshape(1, d), wr, br, tri)


def _route_tables(route_i, counts, n_tokens, n_experts):
    idx = route_i[:, :TOP_K]
    rank = route_i[:, TOP_K:2 * TOP_K]
    n_tiles_max = n_tokens * TOP_K // MOE_TM + n_experts
    s_max = n_experts + n_tiles_max // MOE_NTS
    e_iota = jnp.arange(n_experts, dtype=I32)

    tiles_e = (counts + MOE_TM - 1) // MOE_TM
    tile_end = jnp.cumsum(tiles_e)
    tile_off = tile_end - tiles_e
    off_of = jnp.sum(jnp.where(idx[..., None] == e_iota, tile_off, 0), axis=-1)
    pos = (off_of * MOE_TM + rank).astype(I32)

    nst_e = (tiles_e + MOE_NTS - 1) // MOE_NTS
    st_incl = jnp.cumsum(nst_e)
    num_st = st_incl[-1]
    s = jnp.arange(s_max, dtype=I32)
    active = s < num_st
    s_eff = jnp.minimum(s, num_st - 1)
    e_s = jnp.minimum(jnp.sum(st_incl[None, :] <= s_eff[:, None], axis=1), n_experts - 1).astype(I32)
    onehot = e_s[:, None] == e_iota[None, :]
    pick = lambda v: jnp.sum(jnp.where(onehot, v[None, :], 0), axis=1)
    local = s_eff - (pick(st_incl) - pick(nst_e))
    tile0 = (pick(tile_off) + MOE_NTS * local).astype(I32)
    nt = jnp.where(active, jnp.minimum(MOE_NTS, pick(tiles_e) - MOE_NTS * local), 0).astype(I32)

    tile_ids = jnp.arange(n_tiles_max, dtype=I32)
    is_last = jnp.any((tile_ids[:, None] == tile_end[None, :] - 1) & (tiles_e[None, :] > 0), axis=1)
    pad_tile = (is_last | (tile_ids >= tile_end[-1])).astype(I32)
    return pos, e_s, tile0, nt, pad_tile, tile_end[-1:].astype(I32), num_st.astype(I32)


def _dispatch_kernel(pad_tile, pos_ref, n2_ref, xs_hbm, zero_ref, sem_z, sem_r, *,
                     tokens_per_step, pieces):
    i = pl.program_id(0)
    tile_rows = MOE_TM * pieces
    n_tiles = pad_tile.shape[0]

    @pl.when(i == 0)
    def _zero_fill():
        zero_ref[...] = jnp.zeros_like(zero_ref)

        def fill(tl):
            start = pl.multiple_of(tl * tile_rows, tile_rows)
            return pltpu.make_async_copy(zero_ref, xs_hbm.at[pl.ds(start, tile_rows)], sem_z)

        def issue(tl, c):
            @pl.when(pad_tile[tl] > 0)
            def _():
                fill(tl).start()
            return c

        def drain(tl, c):
            @pl.when(pad_tile[tl] > 0)
            def _():
                fill(tl).wait()
            return c

        lax.fori_loop(0, n_tiles, issue, 0)
        lax.fori_loop(0, n_tiles, drain, 0)

    def row_copy(tok, dst):
        return pltpu.make_async_copy(
            n2_ref.at[pl.ds(pl.multiple_of(tok * pieces, pieces), pieces)],
            xs_hbm.at[pl.ds(pl.multiple_of(dst * pieces, pieces), pieces)], sem_r)

    def token(tok, c):
        for k in range(TOP_K):
            row_copy(tok, pos_ref[tok * TOP_K + k]).start(priority=k % 2)
        return c

    lax.fori_loop(0, tokens_per_step, token, 0, unroll=4)

    def drain(m, c):
        for _ in range(LANES):
            row_copy(0, 0).wait()
        return c

    lax.fori_loop(0, tokens_per_step * TOP_K // LANES, drain, 0)


def _dispatch(n2_slab, pos, pad_tile, d):
    pieces = d // LANES
    n_tokens = n2_slab.shape[0] // pieces
    n_tiles = pad_tile.shape[0]
    tm = min(DISPATCH_TM, n_tokens)
    grid_spec = pltpu.PrefetchScalarGridSpec(
        num_scalar_prefetch=1,
        grid=(n_tokens // tm,),
        in_specs=[pl.BlockSpec((tm * TOP_K,), lambda i, pad: (i,), memory_space=pltpu.SMEM),
                  pl.BlockSpec((tm * pieces, LANES), lambda i, pad: (i, 0))],
        out_specs=pl.BlockSpec(memory_space=pl.ANY),
        scratch_shapes=[pltpu.VMEM((MOE_TM * pieces, LANES), F32),
                        pltpu.SemaphoreType.DMA(()), pltpu.SemaphoreType.DMA(())],
    )
    return pl.pallas_call(
        functools.partial(_dispatch_kernel, tokens_per_step=tm, pieces=pieces),
        grid_spec=grid_spec,
        out_shape=jax.ShapeDtypeStruct((n_tiles * MOE_TM * pieces, LANES), F32),
        compiler_params=_cparams(("arbitrary",)),
        name="dispatch",
    )(pad_tile, pos.reshape(-1), n2_slab)


def _experts_kernel(st_e, st_t0, st_nt, n_used, xs_hbm, wg_ref, wu_ref, wd_ref, bg_ref, bu_ref, bd_ref,
                    y_hbm, tbuf, xs_b, acc, wgb, wub, wdb, sem_g, sem_s, *, n_j, n_tiles):
    s = pl.program_id(0)
    j = pl.program_id(1)
    nt = st_nt[s]
    t0 = st_t0[s]
    pieces = acc.shape[-1] // LANES
    slot_rows = MOE_TM * EXPERT_PITCH

    def hbm_tile(ref, tl):
        return ref.at[pl.ds(pl.multiple_of(tl * MOE_TM, MOE_TM), MOE_TM)]

    def vmem_tile(q):
        slot = tbuf.at[pl.ds(pl.multiple_of((q & 1) * slot_rows, slot_rows), slot_rows)]
        return slot.reshape(MOE_TM, EXPERT_PITCH, LANES).at[:, pl.ds(0, pieces), :]

    def tile_in(q):
        return pltpu.make_async_copy(hbm_tile(xs_hbm, t0 + q), vmem_tile(q), sem_g.at[q & 1])

    def tile_out(q, tl):
        return pltpu.make_async_copy(vmem_tile(q), hbm_tile(y_hbm, tl), sem_s.at[q & 1])

    def for_tiles(lo, hi, fn):
        def body(q, c):
            fn(q)
            return c
        lax.fori_loop(lo, hi, body, 0)

    @pl.when((s == 0) & (j == 0))
    def _zero_tail():
        tbuf[pl.ds(0, slot_rows), :] = jnp.zeros((slot_rows, LANES), F32)
        for_tiles(n_used[0], n_tiles, lambda tl: tile_out(0, tl).start())
        for_tiles(n_used[0], n_tiles, lambda tl: tile_out(0, tl).wait())

    @pl.when(nt > 0)
    def _active():
        @pl.when(j == 0)
        def _load():
            tile_in(0).start()

            def load(q):
                @pl.when(q + 1 < nt)
                def _():
                    tile_in(q + 1).start()

                tile_in(q).wait()
                _from_slab(xs_b, q * MOE_TM, tbuf, (q & 1) * MOE_TM, MOE_TM // (2 * SUBLANES), EXPERT_PITCH)
                acc[pl.ds(pl.multiple_of(q * MOE_TM, MOE_TM), MOE_TM), :] = jnp.broadcast_to(
                    bd_ref[...], (MOE_TM, acc.shape[-1]))

            for_tiles(0, nt, load)

        wgb[...] = wg_ref[...].astype(BF16)
        wub[...] = wu_ref[...].astype(BF16)
        wdb[...] = wd_ref[...].astype(BF16)

        def ffn(tile0, n_tiles_chunk):
            rows = n_tiles_chunk * MOE_TM
            r0 = pl.multiple_of(tile0 * MOE_TM, MOE_TM)
            xt = xs_b[pl.ds(r0, rows), :]
            g = jnp.dot(xt, wgb[...], preferred_element_type=F32) + bg_ref[...]
            u = jnp.dot(xt, wub[...], preferred_element_type=F32) + bu_ref[...]
            g = jnp.minimum(g, SWIGLU_LIMIT)
            u = jnp.clip(u, -SWIGLU_LIMIT, SWIGLU_LIMIT)
            hdn = g * jax.nn.sigmoid(SWIGLU_ALPHA * g) * (u + 1.0)
            acc[pl.ds(r0, rows), :] += jnp.dot(hdn.astype(BF16), wdb[...], preferred_element_type=F32)

        big = MOE_CHUNK_TILES
        for_tiles(0, nt // big, lambda c: ffn(c * big, big))
        done = (nt // big) * big
        piece = big // 2
        while piece >= 1:
            @pl.when((nt & piece) != 0)
            def _(done=done, piece=piece):
                ffn(done, piece)

            done = done + (nt & piece)
            piece //= 2

        @pl.when(j == n_j - 1)
        def _store():
            def store(q):
                @pl.when(q >= 2)
                def _():
                    tile_out(q - 2, t0 + q - 2).wait()

                _to_slab(tbuf, (q & 1) * MOE_TM, acc, q * MOE_TM, MOE_TM // SUBLANES, EXPERT_PITCH)
                tile_out(q, t0 + q).start()

            for_tiles(0, nt, store)

            @pl.when(nt >= 2)
            def _():
                tile_out(nt - 2, t0 + nt - 2).wait()

            tile_out(nt - 1, t0 + nt - 1).wait()


def _experts(xs_slab, tables, num_super_tiles, w_gate, b_gate, w_up, b_up, w_down, b_down):
    st_e, st_t0, st_nt, n_used = tables
    ne, d, f = w_gate.shape
    pieces = d // LANES
    n_tiles = xs_slab.shape[0] // (MOE_TM * pieces)
    n_j = f // MOE_TF
    super_rows = MOE_NTS * MOE_TM

    def w_in_map(s, j, st_e, st_t0, st_nt, n_used):
        return (st_e[s], 0, j)

    def w_down_map(s, j, st_e, st_t0, st_nt, n_used):
        return (st_e[s], j, 0)

    def b_down_map(s, j, st_e, st_t0, st_nt, n_used):
        return (st_e[s], 0, 0)

    any_spec = pl.BlockSpec(memory_space=pl.ANY)
    grid_spec = pltpu.PrefetchScalarGridSpec(
        num_scalar_prefetch=4,
        grid=(num_super_tiles, n_j),
        in_specs=[
            any_spec,
            pl.BlockSpec((None, d, MOE_TF), w_in_map),
            pl.BlockSpec((None, d, MOE_TF), w_in_map),
            pl.BlockSpec((None, MOE_TF, d), w_down_map),
            pl.BlockSpec((None, 1, MOE_TF), w_in_map),
            pl.BlockSpec((None, 1, MOE_TF), w_in_map),
            pl.BlockSpec((None, 1, d), b_down_map),
        ],
        out_specs=any_spec,
        scratch_shapes=[
            pltpu.VMEM((2 * MOE_TM * EXPERT_PITCH, LANES), F32),
            pltpu.VMEM((super_rows, d), BF16),
            pltpu.VMEM((super_rows, d), F32),
            pltpu.VMEM((d, MOE_TF), BF16),
            pltpu.VMEM((d, MOE_TF), BF16),
            pltpu.VMEM((MOE_TF, d), BF16),
            pltpu.SemaphoreType.DMA((2,)),
            pltpu.SemaphoreType.DMA((2,)),
        ],
    )
    return pl.pallas_call(
        functools.partial(_experts_kernel, n_j=n_j, n_tiles=n_tiles),
        grid_spec=grid_spec,
        out_shape=jax.ShapeDtypeStruct((n_tiles * MOE_TM, pieces, LANES), F32),
        compiler_params=_cparams(("arbitrary", "arbitrary")),
        name="experts",
    )(st_e, st_t0, st_nt, n_used, xs_slab.reshape(n_tiles * MOE_TM, pieces, LANES),
      w_gate, w_up, w_down, b_gate.reshape(ne, 1, f), b_up.reshape(ne, 1, f),
      b_down.reshape(ne, 1, d)).reshape(xs_slab.shape)


def _combine_kernel(pos_ref, pos_next_ref, h_ref, rf_ref, g_ref, y_hbm, o_ref, yv, acc_ref, sem, *,
                    final_norm):
    i = pl.program_id(0)
    tm, d = h_ref.shape
    pieces = d // LANES
    slot_rows = TOP_K * tm
    slot = i & 1

    def row_copy(src, dst, sl):
        return pltpu.make_async_copy(
            y_hbm.at[pl.ds(pl.multiple_of(src * pieces, pieces), pieces)],
            yv.reshape(2 * slot_rows, COMBINE_PITCH, LANES).at[sl * slot_rows + dst, pl.ds(0, pieces), :],
            sem.at[sl])

    def gather(table, sl):
        def token(tok, c):
            for k in range(TOP_K):
                row_copy(table[tok * TOP_K + k], k * tm + tok, sl).start(priority=k % 2)
            return c

        lax.fori_loop(0, tm, token, 0, unroll=4)

    @pl.when(i == 0)
    def _():
        gather(pos_ref, 0)

    @pl.when(i + 1 < pl.num_programs(0))
    def _():
        gather(pos_next_ref, 1 - slot)

    def drain(m, c):
        for _ in range(LANES):
            row_copy(0, 0, slot).wait()
        return c

    lax.fori_loop(0, slot_rows // LANES, drain, 0)

    def group(g, c):
        r0 = pl.multiple_of(g * SUBLANES, SUBLANES)
        rf = rf_ref[pl.ds(r0, SUBLANES), :]
        gates = [jnp.broadcast_to(rf[:, k:k + 1], (SUBLANES, LANES)) for k in range(TOP_K)]
        for p in range(pieces):
            v = h_ref[pl.ds(r0, SUBLANES), p * LANES:(p + 1) * LANES]
            for k in range(TOP_K):
                row = (slot * slot_rows + k * tm + r0) * COMBINE_PITCH + p
                v = v + gates[k] * yv[pl.ds(row, SUBLANES, stride=COMBINE_PITCH), :]
            acc_ref[pl.ds(r0, SUBLANES), p * LANES:(p + 1) * LANES] = v
        return c

    lax.fori_loop(0, tm // SUBLANES, group, 0, unroll=2)
    o_ref[...] = _rms(acc_ref[...], g_ref[...]) if final_norm else acc_ref[...]


def _combine(h, y_slab, pos, route_f, g, final_norm):
    t, d = h.shape
    pieces = d // LANES
    tm = min(COMBINE_TM, t)
    n_steps = t // tm
    table = lambda index: pl.BlockSpec((tm * TOP_K,), index, memory_space=pltpu.SMEM)
    return pl.pallas_call(
        functools.partial(_combine_kernel, final_norm=final_norm),
        grid=(n_steps,),
        in_specs=[table(lambda i: (i,)),
                  table(lambda i: (jnp.minimum(i + 1, n_steps - 1),)),
                  pl.BlockSpec((tm, d), lambda i: (i, 0)),
                  pl.BlockSpec((tm, LANES), lambda i: (i, 0)),
                  pl.BlockSpec((1, d), lambda i: (0, 0)),
                  pl.BlockSpec(memory_space=pl.ANY)],
        out_specs=pl.BlockSpec((tm, d), lambda i: (i, 0)),
        out_shape=jax.ShapeDtypeStruct((t, d), F32),
        scratch_shapes=[pltpu.VMEM((2 * TOP_K * tm * COMBINE_PITCH, LANES), F32), pltpu.VMEM((tm, d), F32),
                        pltpu.SemaphoreType.DMA((2,))],
        compiler_params=_cparams(("arbitrary",)),
        name="combine",
    )(pos.reshape(-1), pos.reshape(-1), h, route_f, g.reshape(1, d), y_slab)


def kernel(x, norm1_g, w_in, rel_bias, ssm_lambda_re, ssm_lambda_im, ssm_log_step, ssm_b_re, ssm_b_im, ssm_c_re, ssm_c_im, ssm_d, w_ssm_glu, b_ssm_glu, attn_out_g, ssm_out_g, w_out, norm2_g, w_router, b_router, w_gate, b_gate, w_up, b_up, w_down, b_down, norm_f_g):
    b, l, d = x.shape
    depth = w_in.shape[0]
    aw = attn_out_g.shape[-1]
    n_heads = aw // HEAD_DIM
    n_experts = w_router.shape[-1]
    t = b * l
    assert l % ATT_TQ == 0 and l >= ATT_WIN and l % S5_TC == 0
    assert (t * TOP_K) % MOE_TM == 0 and t % DISPATCH_TM == 0

    h = x.reshape(t, d).astype(F32)
    for li in range(depth):
        last = li == depth - 1
        proj = _inproj(h, norm1_g[li], w_in[li].astype(BF16), tm=min(1024, t), tn=1024)
        proj3 = proj.reshape(b, l, -1)
        mixed_a = _attention(proj3, _attn_bias_table(rel_bias[li]), attn_out_g[li], n_heads)
        wbd, cbd, avec = _s5_params(ssm_lambda_re[li], ssm_lambda_im[li], ssm_log_step[li],
                                    ssm_b_re[li], ssm_b_im[li], ssm_c_re[li], ssm_c_im[li])
        mixed_s = _s5(proj3, wbd, cbd, avec, ssm_d[li], w_ssm_glu[li].astype(BF16), b_ssm_glu[li],
                      ssm_out_g[li], u_block_index=3 * aw // (d - aw))
        h, n2_slab, route_i, route_f, cnt = _outproj(
            mixed_a.reshape(t, aw), mixed_s.reshape(t, d - aw), h, w_out[li].astype(BF16),
            norm2_g[li], w_router[li], b_router[li], tm=min(512, t))
        pos, st_e, st_t0, st_nt, pad_tile, n_used, num_st = _route_tables(
            route_i, cnt[0, :n_experts].astype(I32), t, n_experts)
        xs_slab = _dispatch(n2_slab, pos, pad_tile, d)
        ys_slab = _experts(xs_slab, (st_e, st_t0, st_nt, n_used), num_st, w_gate[li], b_gate[li],
                           w_up[li], b_up[li], w_down[li], b_down[li])
        g_fin = norm_f_g if last else jnp.ones((d,), F32)
        h = _combine(h, ys_slab, pos, route_f, g_fin, final_norm=last)
    return h.reshape(b, l, d).astype(x.dtype)
```

```python
import functools
import math

import jax
import jax.numpy as jnp
from jax import lax
from jax.experimental import pallas as pl
from jax.experimental.pallas import tpu as pltpu

F32 = jnp.float32
BF16 = jnp.bfloat16
I32 = jnp.int32

CHUNK = 64
HEAD_DIM = 128
N_BACK_CHUNKS = 8
REL_CLIP = 128
SSM_GROUP = 16
SSM_STATE = 64
TOP_K = 4
SWIGLU_LIMIT = 7.0
SWIGLU_ALPHA = 1.702
EPS = 1e-5
NEG_BIG = -1e30

LANES = 128
SUBLANES = 8
MXU_DIM = 256
VMEM_LIMIT_BYTES = 56 * 1024 * 1024

ATT_TQ = 2 * CHUNK
ATT_WIN = (N_BACK_CHUNKS + 2) * CHUNK
ATT_NSHIFT = (N_BACK_CHUNKS * CHUNK) // ATT_TQ + 1

S5_GROUPS_PER_BLOCK = MXU_DIM // SSM_GROUP
S5_BLOCK_STATES = S5_GROUPS_PER_BLOCK * SSM_STATE
S5_STRIP = 512
S5_TC = 32

MOE_TM = 256
MOE_NTS = 9
MOE_TF = 256
MOE_CHUNK_TILES = 4
COMBINE_PITCH = 20
EXPERT_PITCH = 20
DISPATCH_TM = 512
COMBINE_TM = 256


def _cparams(sem):
    return pltpu.CompilerParams(dimension_semantics=sem, vmem_limit_bytes=VMEM_LIMIT_BYTES)


def _rms(x, g):
    r = lax.rsqrt(jnp.mean(x * x, axis=-1, keepdims=True) + EPS)
    return x * r * g


def _to_slab(slab_ref, slab_row0, src_ref, src_row0, n_groups, pitch=None):
    pieces = src_ref.shape[-1] // LANES
    pitch = pitch or pieces

    def group(g, c):
        r0 = g * SUBLANES
        src0 = pl.multiple_of(src_row0 + r0, SUBLANES)
        for p in range(pieces):
            slab_ref[pl.ds((slab_row0 + r0) * pitch + p, SUBLANES, stride=pitch), :] = (
                src_ref[pl.ds(src0, SUBLANES), p * LANES:(p + 1) * LANES])
        return c

    lax.fori_loop(0, n_groups, group, 0)


def _from_slab(dst_ref, dst_row0, slab_ref, slab_row0, n_groups, pitch=None):
    pieces = dst_ref.shape[-1] // LANES
    pitch = pitch or pieces
    rows = 2 * SUBLANES

    def group(g, c):
        r0 = g * rows
        dst0 = pl.multiple_of(dst_row0 + r0, rows)
        for p in range(pieces):
            lo = slab_ref[pl.ds((slab_row0 + r0) * pitch + p, SUBLANES, stride=pitch), :]
            hi = slab_ref[pl.ds((slab_row0 + r0 + SUBLANES) * pitch + p, SUBLANES, stride=pitch), :]
            dst_ref[pl.ds(dst0, rows), p * LANES:(p + 1) * LANES] = (
                jnp.concatenate([lo, hi], axis=0).astype(dst_ref.dtype))
        return c

    lax.fori_loop(0, n_groups, group, 0)


def _inproj_kernel(x_ref, g_ref, w_ref, o_ref, nb_ref):
    @pl.when(pl.program_id(1) == 0)
    def _():
        nb_ref[...] = _rms(x_ref[...], g_ref[...]).astype(BF16)

    o_ref[...] = jnp.dot(nb_ref[...], w_ref[...], preferred_element_type=F32).astype(o_ref.dtype)


def _inproj(x2d, g, w_bf16, tm, tn):
    t, d = x2d.shape
    n = w_bf16.shape[1]
    return pl.pallas_call(
        _inproj_kernel,
        grid=(t // tm, n // tn),
        in_specs=[
            pl.BlockSpec((tm, d), lambda i, j: (i, 0)),
            pl.BlockSpec((1, d), lambda i, j: (0, 0)),
            pl.BlockSpec((d, tn), lambda i, j: (0, j)),
        ],
        out_specs=pl.BlockSpec((tm, tn), lambda i, j: (i, j)),
        out_shape=jax.ShapeDtypeStruct((t, n), BF16),
        scratch_shapes=[pltpu.VMEM((tm, d), BF16)],
        compiler_params=_cparams(("arbitrary", "arbitrary")),
        name="inproj",
    )(x2d, g.reshape(1, d), w_bf16)


def _attn_bias_table(rel_bias):
    n_heads = rel_bias.shape[0]
    period = ATT_WIN + ATT_TQ
    m = jnp.arange(period)
    diff = jnp.where(m < ATT_WIN, m, m - period)
    sh = jnp.arange(ATT_NSHIFT)[:, None]
    rel = N_BACK_CHUNKS * CHUNK - ATT_TQ * sh - diff[None, :]
    idx = jnp.clip(rel, -REL_CLIP, REL_CLIP) + REL_CLIP
    vext = rel_bias.astype(F32)[:, idx]
    flat = jnp.tile(vext, (1, 1, ATT_TQ))[..., :ATT_TQ * (period - 1)]
    tbl = flat.reshape(n_heads, ATT_NSHIFT, ATT_TQ, period - 1)[..., :ATT_WIN]
    return jnp.transpose(tbl, (1, 0, 3, 2))


def _attn_kernel(q_ref, k_ref, v_ref, b_ref, g_ref, o_ref, a_ref, *, n_heads):
    i = pl.program_id(1)
    s0 = pl.multiple_of(jnp.maximum(i * ATT_TQ - N_BACK_CHUNKS * CHUNK, 0), ATT_TQ)
    key = lax.broadcasted_iota(I32, (ATT_WIN, ATT_TQ), 0)
    qry = lax.broadcasted_iota(I32, (ATT_WIN, ATT_TQ), 1)
    qc = (i * ATT_TQ + qry) // CHUNK
    kc = (s0 + key) // CHUNK
    ok = (kc <= qc) & (kc >= qc - N_BACK_CHUNKS)
    scale = 1.0 / math.sqrt(HEAD_DIM)
    for h in range(n_heads):
        hs = slice(h * HEAD_DIM, (h + 1) * HEAD_DIM)
        qh = q_ref[:, hs]
        kh = k_ref[pl.ds(s0, ATT_WIN), hs]
        vh = v_ref[pl.ds(s0, ATT_WIN), hs]
        s = lax.dot_general(kh, qh, (((1,), (1,)), ((), ())), preferred_element_type=F32)
        s = jnp.where(ok, s * scale + b_ref[h], NEG_BIG)
        m = jnp.max(s, axis=0, keepdims=True)
        p = jnp.exp(s - m)
        l = jnp.sum(p, axis=0, keepdims=True)
        o_t = lax.dot_general(vh, p.astype(BF16), (((0,), (0,)), ((), ())), preferred_element_type=F32)
        a_ref[:, hs] = (o_t / l).T
    o_ref[...] = _rms(a_ref[...], g_ref[...]).astype(o_ref.dtype)


def _attention(proj3, bias_tbl, g, n_heads):
    b, l, _ = proj3.shape
    aw = n_heads * HEAD_DIM
    nsh = ATT_NSHIFT
    return pl.pallas_call(
        functools.partial(_attn_kernel, n_heads=n_heads),
        grid=(b, l // ATT_TQ),
        in_specs=[
            pl.BlockSpec((None, ATT_TQ, aw), lambda bi, i: (bi, i, 0)),
            pl.BlockSpec((None, l, aw), lambda bi, i: (bi, 0, 1)),
            pl.BlockSpec((None, l, aw), lambda bi, i: (bi, 0, 2)),
            pl.BlockSpec((None, n_heads, ATT_WIN, ATT_TQ),
                         lambda bi, i: (jnp.maximum(nsh - 1 - i, 0), 0, 0, 0)),
            pl.BlockSpec((1, aw), lambda bi, i: (0, 0)),
        ],
        out_specs=pl.BlockSpec((None, ATT_TQ, aw), lambda bi, i: (bi, i, 0)),
        out_shape=jax.ShapeDtypeStruct((b, l, aw), BF16),
        scratch_shapes=[pltpu.VMEM((ATT_TQ, aw), F32)],
        compiler_params=_cparams(("arbitrary", "arbitrary")),
        name="attn",
    )(proj3, proj3, proj3, bias_tbl, g.reshape(1, aw))


def _s5_params(lam_re, lam_im, log_step, b_re, b_im, c_re, c_im):
    g, p = lam_re.shape
    nb = g // S5_GROUPS_PER_BLOCK
    gl = S5_GROUPS_PER_BLOCK
    dt = jnp.exp(log_step.astype(F32))[:, None]
    lr = lam_re.astype(F32)
    li = lam_im.astype(F32)
    mag = jnp.exp(lr * dt)
    ab_re = mag * jnp.cos(li * dt)
    ab_im = mag * jnp.sin(li * dt)
    den = lr * lr + li * li
    nr = ab_re - 1.0
    ni = ab_im
    f_re = (nr * lr + ni * li) / den
    f_im = (ni * lr - nr * li) / den
    br = b_re.astype(F32)
    bi = b_im.astype(F32)
    bb_re = f_re[..., None] * br - f_im[..., None] * bi
    bb_im = f_re[..., None] * bi + f_im[..., None] * br
    eye = jnp.eye(gl, dtype=F32)
    bb = jnp.stack([bb_re, bb_im]).reshape(2, nb, gl, p, SSM_GROUP)
    wbd = jnp.einsum("ajgpc,gh->jgcahp", bb, eye).reshape(nb, gl * SSM_GROUP, 2 * gl * p)
    cc = jnp.stack([c_re.astype(F32), -c_im.astype(F32)]).reshape(2, nb, gl, SSM_GROUP, p)
    cbd = jnp.einsum("ajgcp,gh->jagphc", cc, eye).reshape(nb, 2 * gl * p, gl * SSM_GROUP)
    avec = jnp.stack([ab_re, ab_im]).reshape(2, nb, gl, p)
    avec = jnp.transpose(avec, (1, 0, 2, 3)).reshape(1, 2 * g * p)
    return wbd.astype(BF16), cbd.astype(BF16), avec


def _s5_kernel(u_ref, pm_ref, pmt_ref, wbd_ref, cbd_ref, av_ref, d_ref, wglu_ref, bglu_ref,
               g_ref, o_ref, xs_ref, hst_ref, *, n_blocks, tc, batch):
    rows = tc * batch
    sw = u_ref.shape[-1]
    bw = 2 * S5_BLOCK_STATES
    cw = S5_GROUPS_PER_BLOCK * SSM_GROUP

    @pl.when(pl.program_id(0) == 0)
    def _():
        hst_ref[...] = jnp.zeros_like(hst_ref)

    u_tm = jnp.dot(pm_ref[...], u_ref[...].reshape(rows, sw), preferred_element_type=F32)
    u_tmb = u_tm.astype(BF16)
    for j in range(n_blocks):
        xs_ref[:, j * bw:(j + 1) * bw] = jnp.dot(
            u_tmb[:, j * cw:(j + 1) * cw], wbd_ref[j], preferred_element_type=F32)

    for j in range(n_blocks):
        for s in range(S5_BLOCK_STATES // S5_STRIP):
            cr = j * bw + s * S5_STRIP
            ci = cr + S5_BLOCK_STATES
            a_re = jnp.broadcast_to(av_ref[:, cr:cr + S5_STRIP], (batch, S5_STRIP))
            a_im = jnp.broadcast_to(av_ref[:, ci:ci + S5_STRIP], (batch, S5_STRIP))

            def step(t, carry, cr=cr, ci=ci, a_re=a_re, a_im=a_im):
                h_re, h_im = carry
                r0 = pl.multiple_of(t * batch, batch)
                n_re = a_re * h_re - a_im * h_im + xs_ref[pl.ds(r0, batch), cr:cr + S5_STRIP]
                n_im = a_re * h_im + a_im * h_re + xs_ref[pl.ds(r0, batch), ci:ci + S5_STRIP]
                xs_ref[pl.ds(r0, batch), cr:cr + S5_STRIP] = n_re
                xs_ref[pl.ds(r0, batch), ci:ci + S5_STRIP] = n_im
                return n_re, n_im

            h_re, h_im = lax.fori_loop(
                0, tc, step, (hst_ref[:, cr:cr + S5_STRIP], hst_ref[:, ci:ci + S5_STRIP]), unroll=4)
            hst_ref[:, cr:cr + S5_STRIP] = h_re
            hst_ref[:, ci:ci + S5_STRIP] = h_im

    ys = [jnp.dot(xs_ref[:, j * bw:(j + 1) * bw].astype(BF16), cbd_ref[j], preferred_element_type=F32)
          for j in range(n_blocks)]
    y = jnp.concatenate(ys, axis=1) + d_ref[...] * u_tm
    y = jax.nn.gelu(y)
    z = jnp.dot(y.astype(BF16), wglu_ref[...], preferred_element_type=F32) + bglu_ref[...]
    y = y * jax.nn.sigmoid(z)
    res = _rms(y, g_ref[...]).astype(BF16)
    out_bm = jnp.dot(pmt_ref[...], res, preferred_element_type=F32).astype(o_ref.dtype)
    o_ref[...] = out_bm.reshape(batch, tc, sw)


def _s5(proj3, wbd, cbd, avec, d, wglu_bf16, bglu, g, u_block_index):
    b, l, _ = proj3.shape
    assert b == SUBLANES, "the scan keeps the batch on the sublane axis"
    n_blocks, cw, bw = wbd.shape
    sw = n_blocks * cw
    tc = S5_TC
    rows = tc * b
    r = jnp.arange(rows)
    src = (r % b) * tc + r // b
    pm = (src[:, None] == jnp.arange(rows)[None, :]).astype(BF16)
    pmt = pm.T
    const = lambda shape: pl.BlockSpec(shape, lambda c: (0,) * len(shape))
    return pl.pallas_call(
        functools.partial(_s5_kernel, n_blocks=n_blocks, tc=tc, batch=b),
        grid=(l // tc,),
        in_specs=[
            pl.BlockSpec((b, tc, sw), lambda c: (0, c, u_block_index)),
            const((rows, rows)), const((rows, rows)),
            const((n_blocks, cw, bw)), const((n_blocks, bw, cw)),
            const((1, n_blocks * bw)), const((1, sw)),
            const((sw, sw)), const((1, sw)), const((1, sw)),
        ],
        out_specs=pl.BlockSpec((b, tc, sw), lambda c: (0, c, 0)),
        out_shape=jax.ShapeDtypeStruct((b, l, sw), BF16),
        scratch_shapes=[pltpu.VMEM((rows, n_blocks * bw), F32),
                        pltpu.VMEM((b, n_blocks * bw), F32)],
        compiler_params=_cparams(("arbitrary",)),
        name="s5",
    )(proj3, pm, pmt, wbd, cbd, avec, d.reshape(1, sw), wglu_bf16, bglu.reshape(1, sw),
      g.reshape(1, sw))


def _outproj_kernel(ma_ref, ms_ref, x_ref, wo_ref, g2_ref, wr_ref, br_ref, tri_ref,
                    h_ref, n2s_ref, ri_ref, rf_ref, cnt_ref, carry_ref, n2_ref):
    aw = ma_ref.shape[-1]

    @pl.when(pl.program_id(0) == 0)
    def _():
        carry_ref[...] = jnp.zeros_like(carry_ref)

    h = (x_ref[...]
         + jnp.dot(ma_ref[...], wo_ref[:aw, :], preferred_element_type=F32)
         + jnp.dot(ms_ref[...], wo_ref[aw:, :], preferred_element_type=F32))
    h_ref[...] = h
    n2 = _rms(h, g2_ref[...])
    n2_ref[...] = n2
    _to_slab(n2s_ref, 0, n2_ref, 0, n2.shape[0] // SUBLANES)

    n_hi = n2.astype(BF16)
    n_lo = (n2 - n_hi.astype(F32)).astype(BF16)
    w = wr_ref[...]
    w_hi = w.astype(BF16)
    w_lo = (w - w_hi.astype(F32)).astype(BF16)
    logits = (jnp.dot(n_hi, w_hi, preferred_element_type=F32)
              + jnp.dot(n_lo, w_hi, preferred_element_type=F32)
              + jnp.dot(n_hi, w_lo, preferred_element_type=F32)) + br_ref[...]

    tm, ne = logits.shape
    lane = lax.broadcasted_iota(I32, (tm, ne), 1)
    work = logits
    vals, idxs = [], []
    for _ in range(TOP_K):
        m = jnp.max(work, axis=-1, keepdims=True)
        ik = jnp.min(jnp.where(work == m, lane, ne), axis=-1, keepdims=True)
        vals.append(m)
        idxs.append(ik)
        work = jnp.where(lane == ik, -jnp.inf, work)
    es = [jnp.exp(v - vals[0]) for v in vals]
    den = es[0] + es[1] + es[2] + es[3]

    hot = jnp.zeros((tm, ne), F32)
    for ik in idxs:
        hot = hot + (lane == ik).astype(F32)
    before = jnp.dot(tri_ref[...], hot.astype(BF16), preferred_element_type=F32) + carry_ref[...]
    ri = jnp.zeros((tm, ne), I32)
    rf = jnp.zeros((tm, ne), F32)
    for k in range(TOP_K):
        rank = jnp.sum(jnp.where(lane == idxs[k], before, 0.0), axis=-1, keepdims=True)
        ri = jnp.where(lane == k, idxs[k], ri)
        ri = jnp.where(lane == TOP_K + k, rank.astype(I32), ri)
        rf = jnp.where(lane == k, es[k] / den, rf)
    ri_ref[...] = ri
    rf_ref[...] = rf
    carry_ref[...] = carry_ref[...] + jnp.sum(hot, axis=0, keepdims=True)
    cnt_ref[...] = carry_ref[...]


def _outproj(ma, ms, x2d, wo_bf16, g2, w_router, b_router, tm):
    t, d = x2d.shape
    aw = ma.shape[1]
    ne = w_router.shape[1]
    pieces = d // LANES
    wr = jnp.zeros((d, LANES), F32).at[:, :ne].set(w_router.astype(F32))
    br = jnp.full((1, LANES), NEG_BIG, F32).at[0, :ne].set(b_router.astype(F32))
    tri = (jnp.arange(tm)[:, None] > jnp.arange(tm)[None, :]).astype(BF16)
    const = lambda shape: pl.BlockSpec(shape, lambda i: (0,) * len(shape))
    row = lambda w: pl.BlockSpec((tm, w), lambda i: (i, 0))
    return pl.pallas_call(
        _outproj_kernel,
        grid=(t // tm,),
        in_specs=[row(aw), row(ms.shape[1]), row(d), const((d, d)), const((1, d)),
                  const((d, LANES)), const((1, LANES)), const((tm, tm))],
        out_specs=[row(d), pl.BlockSpec((tm * pieces, LANES), lambda i: (i, 0)),
                   row(LANES), row(LANES), const((1, LANES))],
        out_shape=[jax.ShapeDtypeStruct((t, d), F32), jax.ShapeDtypeStruct((t * pieces, LANES), F32),
                   jax.ShapeDtypeStruct((t, LANES), I32), jax.ShapeDtypeStruct((t, LANES), F32),
                   jax.ShapeDtypeStruct((1, LANES), F32)],
        scratch_shapes=[pltpu.VMEM((1, LANES), F32), pltpu.VMEM((tm, d), F32)],
        compiler_params=_cparams(("arbitrary",)),
        name="outproj",
    )(ma, ms, x2d, wo_bf16, g2.reshape(1, d), wr, br, tri)


def _route_tables(route_i, counts, n_tokens, n_experts):
    idx = route_i[:, :TOP_K]
    rank = route_i[:, TOP_K:2 * TOP_K]
    n_tiles_max = n_tokens * TOP_K // MOE_TM + n_experts
    s_max = n_experts + n_tiles_max // MOE_NTS
    e_iota = jnp.arange(n_experts, dtype=I32)

    tiles_e = (counts + MOE_TM - 1) // MOE_TM
    tile_end = jnp.cumsum(tiles_e)
    tile_off = tile_end - tiles_e
    off_of = jnp.sum(jnp.where(idx[..., None] == e_iota, tile_off, 0), axis=-1)
    pos = (off_of * MOE_TM + rank).astype(I32)

    nst_e = (tiles_e + MOE_NTS - 1) // MOE_NTS
    st_incl = jnp.cumsum(nst_e)
    num_st = st_incl[-1]
    s = jnp.arange(s_max, dtype=I32)
    active = s < num_st
    s_eff = jnp.minimum(s, num_st - 1)
    e_s = jnp.minimum(jnp.sum(st_incl[None, :] <= s_eff[:, None], axis=1), n_experts - 1).astype(I32)
    onehot = e_s[:, None] == e_iota[None, :]
    pick = lambda v: jnp.sum(jnp.where(onehot, v[None, :], 0), axis=1)
    local = s_eff - (pick(st_incl) - pick(nst_e))
    tile0 = (pick(tile_off) + MOE_NTS * local).astype(I32)
    nt = jnp.where(active, jnp.minimum(MOE_NTS, pick(tiles_e) - MOE_NTS * local), 0).astype(I32)

    tile_ids = jnp.arange(n_tiles_max, dtype=I32)
    is_last = jnp.any((tile_ids[:, None] == tile_end[None, :] - 1) & (tiles_e[None, :] > 0), axis=1)
    pad_tile = (is_last | (tile_ids >= tile_end[-1])).astype(I32)
    return pos, e_s, tile0, nt, pad_tile, tile_end[-1:].astype(I32), num_st.astype(I32)


def _dispatch_kernel(pad_tile, pos_ref, n2_ref, xs_hbm, zero_ref, sem_z, sem_r, *,
                     tokens_per_step, pieces):
    i = pl.program_id(0)
    tile_rows = MOE_TM * pieces
    n_tiles = pad_tile.shape[0]

    @pl.when(i == 0)
    def _zero_fill():
        zero_ref[...] = jnp.zeros_like(zero_ref)

        def fill(tl):
            start = pl.multiple_of(tl * tile_rows, tile_rows)
            return pltpu.make_async_copy(zero_ref, xs_hbm.at[pl.ds(start, tile_rows)], sem_z)

        def issue(tl, c):
            @pl.when(pad_tile[tl] > 0)
            def _():
                fill(tl).start()
            return c

        def drain(tl, c):
            @pl.when(pad_tile[tl] > 0)
            def _():
                fill(tl).wait()
            return c

        lax.fori_loop(0, n_tiles, issue, 0)
        lax.fori_loop(0, n_tiles, drain, 0)

    def row_copy(tok, dst):
        return pltpu.make_async_copy(
            n2_ref.at[pl.ds(pl.multiple_of(tok * pieces, pieces), pieces)],
            xs_hbm.at[pl.ds(pl.multiple_of(dst * pieces, pieces), pieces)], sem_r)

    def token(tok, c):
        for k in range(TOP_K):
            row_copy(tok, pos_ref[tok * TOP_K + k]).start(priority=k % 2)
        return c

    lax.fori_loop(0, tokens_per_step, token, 0, unroll=4)

    def drain(m, c):
        for _ in range(LANES):
            row_copy(0, 0).wait()
        return c

    lax.fori_loop(0, tokens_per_step * TOP_K // LANES, drain, 0)


def _dispatch(n2_slab, pos, pad_tile, d):
    pieces = d // LANES
    n_tokens = n2_slab.shape[0] // pieces
    n_tiles = pad_tile.shape[0]
    tm = min(DISPATCH_TM, n_tokens)
    grid_spec = pltpu.PrefetchScalarGridSpec(
        num_scalar_prefetch=1,
        grid=(n_tokens // tm,),
        in_specs=[pl.BlockSpec((tm * TOP_K,), lambda i, pad: (i,), memory_space=pltpu.SMEM),
                  pl.BlockSpec((tm * pieces, LANES), lambda i, pad: (i, 0))],
        out_specs=pl.BlockSpec(memory_space=pl.ANY),
        scratch_shapes=[pltpu.VMEM((MOE_TM * pieces, LANES), F32),
                        pltpu.SemaphoreType.DMA(()), pltpu.SemaphoreType.DMA(())],
    )
    return pl.pallas_call(
        functools.partial(_dispatch_kernel, tokens_per_step=tm, pieces=pieces),
        grid_spec=grid_spec,
        out_shape=jax.ShapeDtypeStruct((n_tiles * MOE_TM * pieces, LANES), F32),
        compiler_params=_cparams(("arbitrary",)),
        name="dispatch",
    )(pad_tile, pos.reshape(-1), n2_slab)


def _experts_kernel(st_e, st_t0, st_nt, n_used, xs_hbm, wg_ref, wu_ref, wd_ref, bg_ref, bu_ref, bd_ref,
                    y_hbm, tbuf, xs_b, acc, wgb, wub, wdb, sem_g, sem_s, *, n_j, n_tiles):
    s = pl.program_id(0)
    j = pl.program_id(1)
    nt = st_nt[s]
    t0 = st_t0[s]
    pieces = acc.shape[-1] // LANES
    slot_rows = MOE_TM * EXPERT_PITCH

    def hbm_tile(ref, tl):
        return ref.at[pl.ds(pl.multiple_of(tl * MOE_TM, MOE_TM), MOE_TM)]

    def vmem_tile(q):
        slot = tbuf.at[pl.ds(pl.multiple_of((q & 1) * slot_rows, slot_rows), slot_rows)]
        return slot.reshape(MOE_TM, EXPERT_PITCH, LANES).at[:, pl.ds(0, pieces), :]

    def tile_in(q):
        return pltpu.make_async_copy(hbm_tile(xs_hbm, t0 + q), vmem_tile(q), sem_g.at[q & 1])

    def tile_out(q, tl):
        return pltpu.make_async_copy(vmem_tile(q), hbm_tile(y_hbm, tl), sem_s.at[q & 1])

    def for_tiles(lo, hi, fn):
        def body(q, c):
            fn(q)
            return c
        lax.fori_loop(lo, hi, body, 0)

    @pl.when((s == 0) & (j == 0))
    def _zero_tail():
        tbuf[pl.ds(0, slot_rows), :] = jnp.zeros((slot_rows, LANES), F32)
        for_tiles(n_used[0], n_tiles, lambda tl: tile_out(0, tl).start())
        for_tiles(n_used[0], n_tiles, lambda tl: tile_out(0, tl).wait())

    @pl.when(nt > 0)
    def _active():
        @pl.when(j == 0)
        def _load():
            tile_in(0).start()

            def load(q):
                @pl.when(q + 1 < nt)
                def _():
                    tile_in(q + 1).start()

                tile_in(q).wait()
                _from_slab(xs_b, q * MOE_TM, tbuf, (q & 1) * MOE_TM, MOE_TM // (2 * SUBLANES), EXPERT_PITCH)
                acc[pl.ds(pl.multiple_of(q * MOE_TM, MOE_TM), MOE_TM), :] = jnp.broadcast_to(
                    bd_ref[...], (MOE_TM, acc.shape[-1]))

            for_tiles(0, nt, load)

        wgb[...] = wg_ref[...].astype(BF16)
        wub[...] = wu_ref[...].astype(BF16)
        wdb[...] = wd_ref[...].astype(BF16)

        def ffn(tile0, n_tiles_chunk):
            rows = n_tiles_chunk * MOE_TM
            r0 = pl.multiple_of(tile0 * MOE_TM, MOE_TM)
            xt = xs_b[pl.ds(r0, rows), :]
            g = jnp.dot(xt, wgb[...], preferred_element_type=F32) + bg_ref[...]
            u = jnp.dot(xt, wub[...], preferred_element_type=F32) + bu_ref[...]
            g = jnp.minimum(g, SWIGLU_LIMIT)
            u = jnp.clip(u, -SWIGLU_LIMIT, SWIGLU_LIMIT)
            hdn = g * jax.nn.sigmoid(SWIGLU_ALPHA * g) * (u + 1.0)
            acc[pl.ds(r0, rows), :] += jnp.dot(hdn.astype(BF16), wdb[...], preferred_element_type=F32)

        big = MOE_CHUNK_TILES
        for_tiles(0, nt // big, lambda c: ffn(c * big, big))
        done = (nt // big) * big
        piece = big // 2
        while piece >= 1:
            @pl.when((nt & piece) != 0)
            def _(done=done, piece=piece):
                ffn(done, piece)

            done = done + (nt & piece)
            piece //= 2

        @pl.when(j == n_j - 1)
        def _store():
            def store(q):
                @pl.when(q >= 2)
                def _():
                    tile_out(q - 2, t0 + q - 2).wait()

                _to_slab(tbuf, (q & 1) * MOE_TM, acc, q * MOE_TM, MOE_TM // SUBLANES, EXPERT_PITCH)
                tile_out(q, t0 + q).start()

            for_tiles(0, nt, store)

            @pl.when(nt >= 2)
            def _():
                tile_out(nt - 2, t0 + nt - 2).wait()

            tile_out(nt - 1, t0 + nt - 1).wait()


def _experts(xs_slab, tables, num_super_tiles, w_gate, b_gate, w_up, b_up, w_down, b_down):
    st_e, st_t0, st_nt, n_used = tables
    ne, d, f = w_gate.shape
    pieces = d // LANES
    n_tiles = xs_slab.shape[0] // (MOE_TM * pieces)
    n_j = f // MOE_TF
    super_rows = MOE_NTS * MOE_TM

    def w_in_map(s, j, st_e, st_t0, st_nt, n_used):
        return (st_e[s], 0, j)

    def w_down_map(s, j, st_e, st_t0, st_nt, n_used):
        return (st_e[s], j, 0)

    def b_down_map(s, j, st_e, st_t0, st_nt, n_used):
        return (st_e[s], 0, 0)

    any_spec = pl.BlockSpec(memory_space=pl.ANY)
    grid_spec = pltpu.PrefetchScalarGridSpec(
        num_scalar_prefetch=4,
        grid=(num_super_tiles, n_j),
        in_specs=[
            any_spec,
            pl.BlockSpec((None, d, MOE_TF), w_in_map),
            pl.BlockSpec((None, d, MOE_TF), w_in_map),
            pl.BlockSpec((None, MOE_TF, d), w_down_map),
            pl.BlockSpec((None, 1, MOE_TF), w_in_map),
            pl.BlockSpec((None, 1, MOE_TF), w_in_map),
            pl.BlockSpec((None, 1, d), b_down_map),
        ],
        out_specs=any_spec,
        scratch_shapes=[
            pltpu.VMEM((2 * MOE_TM * EXPERT_PITCH, LANES), F32),
            pltpu.VMEM((super_rows, d), BF16),
            pltpu.VMEM((super_rows, d), F32),
            pltpu.VMEM((d, MOE_TF), BF16),
            pltpu.VMEM((d, MOE_TF), BF16),
            pltpu.VMEM((MOE_TF, d), BF16),
            pltpu.SemaphoreType.DMA((2,)),
            pltpu.SemaphoreType.DMA((2,)),
        ],
    )
    return pl.pallas_call(
        functools.partial(_experts_kernel, n_j=n_j, n_tiles=n_tiles),
        grid_spec=grid_spec,
        out_shape=jax.ShapeDtypeStruct((n_tiles * MOE_TM, pieces, LANES), F32),
        compiler_params=_cparams(("arbitrary", "arbitrary")),
        name="experts",
    )(st_e, st_t0, st_nt, n_used, xs_slab.reshape(n_tiles * MOE_TM, pieces, LANES),
      w_gate, w_up, w_down, b_gate.reshape(ne, 1, f), b_up.reshape(ne, 1, f),
      b_down.reshape(ne, 1, d)).reshape(xs_slab.shape)


def _combine_kernel(pos_ref, pos_next_ref, h_ref, rf_ref, g_ref, y_hbm, o_ref, yv, acc_ref, sem, *,
                    final_norm):
    i = pl.program_id(0)
    tm, d = h_ref.shape
    pieces = d // LANES
    slot_rows = TOP_K * tm
    slot = i & 1

    def row_copy(src, dst, sl):
        return pltpu.make_async_copy(
            y_hbm.at[pl.ds(pl.multiple_of(src * pieces, pieces), pieces)],
            yv.reshape(2 * slot_rows, COMBINE_PITCH, LANES).at[sl * slot_rows + dst, pl.ds(0, pieces), :],
            sem.at[sl])

    def gather(table, sl):
        def token(tok, c):
            for k in range(TOP_K):
                row_copy(table[tok * TOP_K + k], k * tm + tok, sl).start(priority=k % 2)
            return c

        lax.fori_loop(0, tm, token, 0, unroll=4)

    @pl.when(i == 0)
    def _():
        gather(pos_ref, 0)

    @pl.when(i + 1 < pl.num_programs(0))
    def _():
        gather(pos_next_ref, 1 - slot)

    def drain(m, c):
        for _ in range(LANES):
            row_copy(0, 0, slot).wait()
        return c

    lax.fori_loop(0, slot_rows // LANES, drain, 0)

    def group(g, c):
        r0 = pl.multiple_of(g * SUBLANES, SUBLANES)
        rf = rf_ref[pl.ds(r0, SUBLANES), :]
        gates = [jnp.broadcast_to(rf[:, k:k + 1], (SUBLANES, LANES)) for k in range(TOP_K)]
        for p in range(pieces):
            v = h_ref[pl.ds(r0, SUBLANES), p * LANES:(p + 1) * LANES]
            for k in range(TOP_K):
                row = (slot * slot_rows + k * tm + r0) * COMBINE_PITCH + p
                v = v + gates[k] * yv[pl.ds(row, SUBLANES, stride=COMBINE_PITCH), :]
            acc_ref[pl.ds(r0, SUBLANES), p * LANES:(p + 1) * LANES] = v
        return c

    lax.fori_loop(0, tm // SUBLANES, group, 0, unroll=2)
    o_ref[...] = _rms(acc_ref[...], g_ref[...]) if final_norm else acc_ref[...]


def _combine(h, y_slab, pos, route_f, g, final_norm):
    t, d = h.shape
    pieces = d // LANES
    tm = min(COMBINE_TM, t)
    n_steps = t // tm
    table = lambda index: pl.BlockSpec((tm * TOP_K,), index, memory_space=pltpu.SMEM)
    return pl.pallas_call(
        functools.partial(_combine_kernel, final_norm=final_norm),
        grid=(n_steps,),
        in_specs=[table(lambda i: (i,)),
                  table(lambda i: (jnp.minimum(i + 1, n_steps - 1),)),
                  pl.BlockSpec((tm, d), lambda i: (i, 0)),
                  pl.BlockSpec((tm, LANES), lambda i: (i, 0)),
                  pl.BlockSpec((1, d), lambda i: (0, 0)),
                  pl.BlockSpec(memory_space=pl.ANY)],
        out_specs=pl.BlockSpec((tm, d), lambda i: (i, 0)),
        out_shape=jax.ShapeDtypeStruct((t, d), F32),
        scratch_shapes=[pltpu.VMEM((2 * TOP_K * tm * COMBINE_PITCH, LANES), F32), pltpu.VMEM((tm, d), F32),
                        pltpu.SemaphoreType.DMA((2,))],
        compiler_params=_cparams(("arbitrary",)),
        name="combine",
    )(pos.reshape(-1), pos.reshape(-1), h, route_f, g.reshape(1, d), y_slab)


def kernel(x, norm1_g, w_in, rel_bias, ssm_lambda_re, ssm_lambda_im, ssm_log_step, ssm_b_re, ssm_b_im, ssm_c_re, ssm_c_im, ssm_d, w_ssm_glu, b_ssm_glu, attn_out_g, ssm_out_g, w_out, norm2_g, w_router, b_router, w_gate, b_gate, w_up, b_up, w_down, b_down, norm_f_g):
    b, l, d = x.shape
    depth = w_in.shape[0]
    aw = attn_out_g.shape[-1]
    n_heads = aw // HEAD_DIM
    n_experts = w_router.shape[-1]
    t = b * l
    assert l % ATT_TQ == 0 and l >= ATT_WIN and l % S5_TC == 0
    assert (t * TOP_K) % MOE_TM == 0 and t % DISPATCH_TM == 0

    h = x.reshape(t, d).astype(F32)
    for li in range(depth):
        last = li == depth - 1
        proj = _inproj(h, norm1_g[li], w_in[li].astype(BF16), tm=min(1024, t), tn=1024)
        proj3 = proj.reshape(b, l, -1)
        mixed_a = _attention(proj3, _attn_bias_table(rel_bias[li]), attn_out_g[li], n_heads)
        wbd, cbd, avec = _s5_params(ssm_lambda_re[li], ssm_lambda_im[li], ssm_log_step[li],
                                    ssm_b_re[li], ssm_b_im[li], ssm_c_re[li], ssm_c_im[li])
        mixed_s = _s5(proj3, wbd, cbd, avec, ssm_d[li], w_ssm_glu[li].astype(BF16), b_ssm_glu[li],
                      ssm_out_g[li], u_block_index=3 * aw // (d - aw))
        h, n2_slab, route_i, route_f, cnt = _outproj(
            mixed_a.reshape(t, aw), mixed_s.reshape(t, d - aw), h, w_out[li].astype(BF16),
            norm2_g[li], w_router[li], b_router[li], tm=min(512, t))
        pos, st_e, st_t0, st_nt, pad_tile, n_used, num_st = _route_tables(
            route_i, cnt[0, :n_experts].astype(I32), t, n_experts)
        xs_slab = _dispatch(n2_slab, pos, pad_tile, d)
        ys_slab = _experts(xs_slab, (st_e, st_t0, st_nt, n_used), num_st, w_gate[li], b_gate[li],
                           w_up[li], b_up[li], w_down[li], b_down[li])
        g_fin = norm_f_g if last else jnp.ones((d,), F32)
        h = _combine(h, ys_slab, pos, route_f, g_fin, final_norm=last)
    return h.reshape(b, l, d).astype(x.dtype)
```
